```python
import math
import jax, jax.numpy as jnp
from jax import lax
import numpy as np

D_MODEL = 1024
BATCH = 8
SEQ = 4096
DEPTH = 1

M_HEADS = 4
M_HEAD_DIM = 128
M_WIDTH = M_HEADS * M_HEAD_DIM
M_CHUNK = 64
A_HEADS = 8
A_NOPE = 64
A_ROPE = 32
A_V = 64
A_WIDTH = A_HEADS * A_V
Q_RANK = 384
KV_RANK = 256
ROPE_THETA = 10000.0
Q_BLOCK = 128
D_MIX = M_WIDTH + A_WIDTH
D_FF = ((8 * D_MODEL // 3 + 255) // 256) * 256
EPS = 1e-6
IN_SPLITS = (M_WIDTH, M_WIDTH, M_WIDTH, M_WIDTH, M_HEADS, M_HEADS, Q_RANK, KV_RANK, A_ROPE)
D_IN = sum(IN_SPLITS)

kernel_name = "hymba_mlstm_mla_swiglu"


def rmsnorm(x, w):
    xf = x.astype(jnp.float32)
    y = xf * lax.rsqrt(jnp.mean(xf * xf, axis=-1, keepdims=True) + EPS)
    return (y * w.astype(jnp.float32)).astype(x.dtype)


def apply_rope(x, positions):
    half = A_ROPE // 2
    inv_freq = ROPE_THETA ** (-jnp.arange(half, dtype=jnp.float32) / half)
    ang = positions.astype(jnp.float32)[:, :, None, None] * inv_freq
    cos, sin = jnp.cos(ang), jnp.sin(ang)
    xf = x.astype(jnp.float32)
    x1, x2 = xf[..., :half], xf[..., half:]
    return jnp.concatenate([x1 * cos - x2 * sin, x2 * cos + x1 * sin], axis=-1).astype(x.dtype)


def mlstm_chunkwise(q, k, v, i_raw, f_raw):
    B, S, H, Dh = q.shape
    L = M_CHUNK
    NC = S // L
    f32 = jnp.float32

    def to_chunks(t):
        return t.astype(f32).reshape(B, NC, L, H, -1).transpose(0, 3, 1, 2, 4)

    def gate_chunks(t):
        return t.astype(f32).reshape(B, NC, L, H).transpose(0, 3, 1, 2)

    qc = to_chunks(q) * (Dh ** -0.5)
    kc = to_chunks(k)
    vc = to_chunks(v)
    logi = gate_chunks(i_raw)
    logf = jax.nn.log_sigmoid(gate_chunks(f_raw))
    b = jnp.cumsum(logf, axis=-1)
    g = b[..., -1]

    a = g[..., None] - b + logi
    m_loc = jnp.max(a, axis=-1)
    w = jnp.exp(a - m_loc[..., None])
    C_loc = jnp.einsum('bhcl,bhcld,bhcle->bhcde', w, kc, vc)
    n_loc = jnp.einsum('bhcl,bhcld->bhcd', w, kc)

    def step(carry, xs):
        C, n, m = carry
        g_c, m_l, C_l, n_l = xs
        m_new = jnp.maximum(g_c + m, m_l)
        s_old = jnp.exp(g_c + m - m_new)
        s_loc = jnp.exp(m_l - m_new)
        C_new = s_old[..., None, None] * C + s_loc[..., None, None] * C_l
        n_new = s_old[..., None] * n + s_loc[..., None] * n_l
        return (C_new, n_new, m_new), (C, n, m)

    init = (jnp.zeros((B, H, Dh, vc.shape[-1]), f32),
            jnp.zeros((B, H, Dh), f32),
            jnp.zeros((B, H), f32))
    xs = (jnp.moveaxis(g, 2, 0), jnp.moveaxis(m_loc, 2, 0),
          jnp.moveaxis(C_loc, 2, 0), jnp.moveaxis(n_loc, 2, 0))
    _, (C_prev, n_prev, m_prev) = lax.scan(step, init, xs)
    C_prev = jnp.moveaxis(C_prev, 0, 2)
    n_prev = jnp.moveaxis(n_prev, 0, 2)
    m_prev = jnp.moveaxis(m_prev, 0, 2)

    causal = jnp.tril(jnp.ones((L, L), dtype=bool))
    d_log = b[..., :, None] - b[..., None, :] + logi[..., None, :]
    d_log = jnp.where(causal, d_log, -jnp.inf)
    inter_log = b + m_prev[..., None]
    m_t = jnp.maximum(jnp.max(d_log, axis=-1), inter_log)
    p = jnp.exp(d_log - m_t[..., None]) * jnp.einsum('bhcld,bhcsd->bhcls', qc, kc)
    s_inter = jnp.exp(inter_log - m_t)
    num = (jnp.einsum('bhcls,bhcse->bhcle', p, vc)
           + s_inter[..., None] * jnp.einsum('bhcld,bhcde->bhcle', qc, C_prev))
    den = jnp.sum(p, axis=-1) + s_inter * jnp.einsum('bhcld,bhcd->bhcl', qc, n_prev)
    den = jnp.maximum(jnp.abs(den), jnp.exp(-m_t))
    h = num / den[..., None]
    return h.transpose(0, 2, 3, 1, 4).reshape(B, S, H, -1).astype(q.dtype)


def mla_attention(q_nope, q_rope, k_nope, k_rope, v):
    B, S, H, _ = q_nope.shape
    NQ = S // Q_BLOCK
    scale = (A_NOPE + A_ROPE) ** -0.5
    qn = q_nope.reshape(B, NQ, Q_BLOCK, H, A_NOPE).swapaxes(0, 1)
    qr = q_rope.reshape(B, NQ, Q_BLOCK, H, A_ROPE).swapaxes(0, 1)
    kr = k_rope[:, :, 0, :]
    k_pos = jnp.arange(S)

    def block(args):
        idx, qn_b, qr_b = args
        s = (jnp.einsum('bqhd,bkhd->bhqk', qn_b, k_nope)
             + jnp.einsum('bqhd,bkd->bhqk', qr_b, kr)).astype(jnp.float32) * scale
        q_pos = idx * Q_BLOCK + jnp.arange(Q_BLOCK)
        s = jnp.where(k_pos[None, :] <= q_pos[:, None], s, -jnp.inf)
        p = jax.nn.softmax(s, axis=-1).astype(v.dtype)
        return jnp.einsum('bhqk,bkhd->bqhd', p, v)

    out = lax.map(block, (jnp.arange(NQ), qn, qr))
    return out.swapaxes(0, 1).reshape(B, S, H * A_V)


def setup_inputs(seed: int = 0) -> dict:
    key = jax.random.key(seed)
    ks = jax.random.split(key, 20)
    f32 = jnp.float32

    def nrm(k, shape, scale):
        return jax.random.normal(k, shape, f32) * scale

    def gain(k, shape):
        return 1.0 + 0.01 * jax.random.normal(k, shape, f32)

    x = jax.random.normal(ks[0], (BATCH, SEQ, D_MODEL), f32)
    offsets = jax.random.randint(ks[1], (BATCH, 1), 0, 1024, dtype=jnp.int32)
    positions = (jnp.arange(SEQ, dtype=jnp.int32)[None, :] + offsets).astype(jnp.int32)
    i_bias = 0.1 * jax.random.normal(ks[2], (DEPTH, M_HEADS), f32)
    f_bias = jnp.linspace(3.0, 6.0, M_HEADS, dtype=f32)[None, :] + 0.1 * jax.random.normal(ks[3], (DEPTH, M_HEADS), f32)
    return {
        "x": x,
        "positions": positions,
        "attn_norm_w": gain(ks[4], (DEPTH, D_MODEL)),
        "w_in": nrm(ks[5], (DEPTH, D_MODEL, D_IN), D_MODEL ** -0.5),
        "b_gates": jnp.concatenate([i_bias, f_bias], axis=-1),
        "mlstm_norm_w": gain(ks[6], (DEPTH, M_HEADS, M_HEAD_DIM)),
        "q_a_norm_w": gain(ks[7], (DEPTH, Q_RANK)),
        "w_q_b": nrm(ks[8], (DEPTH, Q_RANK, A_HEADS * (A_NOPE + A_ROPE)), Q_RANK ** -0.5),
        "kv_a_norm_w": gain(ks[9], (DEPTH, KV_RANK)),
        "w_kv_b": nrm(ks[10], (DEPTH, KV_RANK, A_HEADS * (A_NOPE + A_V)), KV_RANK ** -0.5),
        "w_out": nrm(ks[11], (DEPTH, D_MIX, D_MODEL), D_MIX ** -0.5),
        "ffn_norm_w": gain(ks[12], (DEPTH, D_MODEL)),
        "w_gate": nrm(ks[13], (DEPTH, D_MODEL, D_FF), D_MODEL ** -0.5),
        "w_up": nrm(ks[14], (DEPTH, D_MODEL, D_FF), D_MODEL ** -0.5),
        "w_down": nrm(ks[15], (DEPTH, D_FF, D_MODEL), D_FF ** -0.5),
        "final_norm_w": gain(ks[16], (D_MODEL,)),
    }


def reference(x, positions, attn_norm_w, w_in, b_gates, mlstm_norm_w, q_a_norm_w, w_q_b,
              kv_a_norm_w, w_kv_b, w_out, ffn_norm_w, w_gate, w_up, w_down, final_norm_w):
    B, S = x.shape[0], x.shape[1]
    split_points = np.cumsum(np.array(IN_SPLITS))[:-1].tolist()
    h = x
    for l in range(DEPTH):
        u = rmsnorm(h, attn_norm_w[l])
        proj = u @ w_in[l]
        mq, mk, mv, mo, mi, mf, qa, kva, kr = jnp.split(proj, split_points, axis=-1)

        mi = mi + b_gates[l, :M_HEADS]
        mf = mf + b_gates[l, M_HEADS:]
        to_heads = lambda t: t.reshape(B, S, M_HEADS, M_HEAD_DIM)
        hm = mlstm_chunkwise(to_heads(mq), to_heads(mk), to_heads(mv), mi, mf)
        hm = rmsnorm(hm, mlstm_norm_w[l]).reshape(B, S, M_WIDTH)
        hm = hm * jax.nn.sigmoid(mo)

        cq = (rmsnorm(qa, q_a_norm_w[l]) @ w_q_b[l]).reshape(B, S, A_HEADS, A_NOPE + A_ROPE)
        q_nope, q_rope = cq[..., :A_NOPE], apply_rope(cq[..., A_NOPE:], positions)
        ckv = (rmsnorm(kva, kv_a_norm_w[l]) @ w_kv_b[l]).reshape(B, S, A_HEADS, A_NOPE + A_V)
        k_nope, v = ckv[..., :A_NOPE], ckv[..., A_NOPE:]
        k_rope = apply_rope(kr[:, :, None, :], positions)
        ha = mla_attention(q_nope, q_rope, k_nope, k_rope, v)

        h = h + jnp.concatenate([hm, ha], axis=-1) @ w_out[l]

        u = rmsnorm(h, ffn_norm_w[l])
        h = h + (jax.nn.silu(u @ w_gate[l]) * (u @ w_up[l])) @ w_down[l]
    return rmsnorm(h, final_norm_w)
```

```python
import functools
import math

import jax
import jax.numpy as jnp
from jax import lax
from jax.experimental import pallas as pl
from jax.experimental.pallas import tpu as pltpu

F32 = jnp.float32
BF16 = jnp.bfloat16

D_MODEL = 1024
M_HEADS = 4
M_HEAD_DIM = 128
M_WIDTH = M_HEADS * M_HEAD_DIM
A_HEADS = 8
A_NOPE = 64
A_ROPE = 32
A_V = 64
A_WIDTH = A_HEADS * A_V
Q_RANK = 384
KV_RANK = 256
ROPE_THETA = 10000.0
D_FF = 2816
EPS = 1e-6

LANE = 128
HEAD_PAD = LANE
A_PAD = A_HEADS * HEAD_PAD
MISC = LANE
D_IN_PAD = 4 * M_WIDTH + Q_RANK + KV_RANK + MISC
VMEM_LIMIT = 56 * 1024 * 1024

TM_PROJ = 512
TM_FFN = 512
M_CHUNK = 128
TQ = 256
FF_CHUNK = 256

M_SCALE = M_HEAD_DIM ** -0.5
A_SCALE2 = (A_NOPE + A_ROPE) ** -0.5 * math.log2(math.e)
NEG = -1e30


def _rms(x, w):
    return x * lax.rsqrt(jnp.mean(x * x, axis=-1, keepdims=True) + EPS) * w


def _proj_kernel(x_ref, pos_ref, nw_ref, w1_ref, qnw_ref, wq_ref, kvnw_ref, wkv_ref, tab_ref,
                 mls_ref, g_ref, q_ref, k_ref, v_ref):
    x = x_ref[...]
    u = _rms(x, nw_ref[...]).astype(BF16)

    pa = jnp.dot(u, w1_ref[:, 0:4 * M_WIDTH], preferred_element_type=F32)
    mls_ref[:, 0:M_WIDTH] = (pa[:, 0:M_WIDTH] * M_SCALE).astype(BF16)
    mls_ref[:, M_WIDTH:4 * M_WIDTH] = pa[:, M_WIDTH:4 * M_WIDTH].astype(BF16)

    pb = jnp.dot(u, w1_ref[:, 4 * M_WIDTH:D_IN_PAD], preferred_element_type=F32)
    qa = pb[:, 0:Q_RANK]
    kva = pb[:, Q_RANK:Q_RANK + KV_RANK]
    misc = pb[:, Q_RANK + KV_RANK:Q_RANK + KV_RANK + MISC]
    g_ref[...] = misc

    inv_freq = tab_ref[0:1, :]
    ang = pos_ref[...].astype(F32) * inv_freq
    cos = jnp.cos(ang)
    sin = jnp.sin(ang)
    qa_tab = cos * tab_ref[1:2, :]
    qb_tab = sin * tab_ref[2:3, :]
    ka_tab = cos * tab_ref[3:4, :]
    kb_tab = sin * tab_ref[4:5, :]

    kr = misc * ka_tab + pltpu.roll(misc, 96, 1) * kb_tab

    cq = jnp.dot(_rms(qa, qnw_ref[...]).astype(BF16), wq_ref[...], preferred_element_type=F32)
    ckv = jnp.dot(_rms(kva, kvnw_ref[...]).astype(BF16), wkv_ref[...], preferred_element_type=F32)
    v_one = tab_ref[5:6, :]
    for h in range(A_HEADS):
        sl = slice(h * HEAD_PAD, (h + 1) * HEAD_PAD)
        qh = cq[:, sl]
        q_ref[:, sl] = (qh * qa_tab + pltpu.roll(qh, 96, 1) * qb_tab).astype(BF16)
        k_ref[:, sl] = (ckv[:, sl] + kr).astype(BF16)
        v_ref[:, sl] = (ckv[:, A_PAD + h * HEAD_PAD:A_PAD + (h + 1) * HEAD_PAD] + v_one).astype(BF16)


def _proj_call(x2, pos2, nw, w1, qnw, wq, kvnw, wkv, tab):
    T = x2.shape[0]
    tm = TM_PROJ
    const = lambda i: (0, 0)
    tok = lambda i: (i, 0)
    return pl.pallas_call(
        _proj_kernel,
        grid=(T // tm,),
        in_specs=[
            pl.BlockSpec((tm, D_MODEL), tok),
            pl.BlockSpec((tm, 1), tok),
            pl.BlockSpec((1, D_MODEL), const),
            pl.BlockSpec((D_MODEL, D_IN_PAD), const),
            pl.BlockSpec((1, Q_RANK), const),
            pl.BlockSpec((Q_RANK, A_PAD), const),
            pl.BlockSpec((1, KV_RANK), const),
            pl.BlockSpec((KV_RANK, 2 * A_PAD), const),
            pl.BlockSpec((8, LANE), const),
        ],
        out_specs=[
            pl.BlockSpec((tm, 4 * M_WIDTH), tok),
            pl.BlockSpec((tm, MISC), tok),
            pl.BlockSpec((tm, A_PAD), tok),
            pl.BlockSpec((tm, A_PAD), tok),
            pl.BlockSpec((tm, A_PAD), tok),
        ],
        out_shape=[
            jax.ShapeDtypeStruct((T, 4 * M_WIDTH), BF16),
            jax.ShapeDtypeStruct((T, MISC), F32),
            jax.ShapeDtypeStruct((T, A_PAD), BF16),
            jax.ShapeDtypeStruct((T, A_PAD), BF16),
            jax.ShapeDtypeStruct((T, A_PAD), BF16),
        ],
        compiler_params=pltpu.CompilerParams(
            dimension_semantics=("arbitrary",), vmem_limit_bytes=VMEM_LIMIT),
        name="proj",
    )(x2, pos2, nw, w1, qnw, wq, kvnw, wkv, tab)


def _mlstm_kernel(q_ref, k_ref, v_ref, o_ref, g_ref, bias_ref, nw_ref, out_ref, c_ref, m_ref):
    L = M_CHUNK

    @pl.when(pl.program_id(1) == 0)
    def _():
        c_ref[...] = jnp.zeros_like(c_ref)
        m_ref[...] = jnp.zeros_like(m_ref)

    x = g_ref[...] + bias_ref[...]
    lf = jax.nn.log_sigmoid(x)
    row = lax.broadcasted_iota(jnp.int32, (L, L), 0)
    col = lax.broadcasted_iota(jnp.int32, (L, L), 1)
    causal = col <= row
    tril = causal.astype(F32)
    bc = jnp.dot(tril, lf, preferred_element_type=F32, precision=lax.Precision.HIGHEST)
    bct = bc.T
    xt = x.T
    ones = jnp.ones((L, M_HEAD_DIM), BF16)

    for h in range(M_HEADS):
        sl = slice(h * M_HEAD_DIM, (h + 1) * M_HEAD_DIM)
        b_col = bc[:, M_HEADS + h:M_HEADS + h + 1]
        b_row = bct[M_HEADS + h:M_HEADS + h + 1, :]
        li_row = xt[h:h + 1, :]
        g_tot = b_col[L - 1:L, :]
        m_prev = m_ref[h][0:1, 0:1]

        qh = q_ref[:, sl]
        kt = k_ref[:, sl].astype(F32).T
        vx = jnp.concatenate([v_ref[:, sl], ones], axis=1)

        a_row = g_tot - b_row + li_row
        m_loc = jnp.max(a_row, axis=1, keepdims=True)
        w_row = jnp.exp(a_row - m_loc)
        c_loc = jnp.dot((kt * w_row).astype(BF16), vx, preferred_element_type=F32)

        d_log = jnp.where(causal, b_col - b_row + li_row, NEG)
        inter_log = b_col + m_prev
        m_t = jnp.maximum(jnp.max(d_log, axis=1, keepdims=True), inter_log)
        s = jnp.dot(qh, kt.astype(BF16), preferred_element_type=F32)
        p = jnp.exp(d_log - m_t) * s
        s_inter = jnp.exp(inter_log - m_t)
        c_prev = c_ref[h]
        pv = jnp.dot(p.astype(BF16), vx, preferred_element_type=F32)
        qc = jnp.dot(qh, c_prev.astype(BF16), preferred_element_type=F32)
        num = pv[:, 0:M_HEAD_DIM] + s_inter * qc[:, 0:M_HEAD_DIM]
        den = pv[:, M_HEAD_DIM:M_HEAD_DIM + 1] + s_inter * qc[:, M_HEAD_DIM:M_HEAD_DIM + 1]
        den = jnp.maximum(jnp.abs(den), jnp.exp(-m_t))
        hh = num / den
        hn = _rms(hh, nw_ref[:, sl])
        out_ref[:, sl] = (hn * jax.nn.sigmoid(o_ref[:, sl].astype(F32))).astype(BF16)

        m_new = jnp.maximum(g_tot + m_prev, m_loc)
        s_old = jnp.exp(g_tot + m_prev - m_new)
        s_loc = jnp.exp(m_loc - m_new)
        c_ref[h] = s_old * c_prev + s_loc * c_loc
        m_ref[h] = jnp.broadcast_to(m_new, (8, LANE))


def _mlstm_call(mls, gates, bias, nw, B, S):
    T = B * S
    L = M_CHUNK
    nc = S // L
    blk = lambda j: (lambda b, c: (b * nc + c, j))
    const = lambda b, c: (0, 0)
    return pl.pallas_call(
        _mlstm_kernel,
        grid=(B, nc),
        in_specs=[
            pl.BlockSpec((L, M_WIDTH), blk(0)),
            pl.BlockSpec((L, M_WIDTH), blk(1)),
            pl.BlockSpec((L, M_WIDTH), blk(2)),
            pl.BlockSpec((L, M_WIDTH), blk(3)),
            pl.BlockSpec((L, MISC), blk(0)),
            pl.BlockSpec((1, MISC), const),
            pl.BlockSpec((1, M_WIDTH), const),
        ],
        out_specs=pl.BlockSpec((L, M_WIDTH), blk(0)),
        out_shape=jax.ShapeDtypeStruct((T, M_WIDTH), BF16),
        scratch_shapes=[
            pltpu.VMEM((M_HEADS, M_HEAD_DIM, 2 * M_HEAD_DIM), F32),
            pltpu.VMEM((M_HEADS, 8, LANE), F32),
        ],
        compiler_params=pltpu.CompilerParams(
            dimension_semantics=("arbitrary", "arbitrary"), vmem_limit_bytes=VMEM_LIMIT),
        name="mlstm",
    )(mls, mls, mls, mls, gates, bias, nw)


def _attn_kernel(q_ref, k_ref, v_ref, o_ref):
    i = pl.program_id(2)
    tq = TQ
    nt = (((1,), (1,)), ((), ()))
    qs = [q_ref[:, h * HEAD_PAD:(h + 1) * HEAD_PAD] for h in range(2)]

    def step(start, carry, masked):
        new = []
        for h in range(2):
            m, acc = carry[h]
            kj = k_ref[pl.ds(start, tq), h * HEAD_PAD:(h + 1) * HEAD_PAD]
            vj = v_ref[pl.ds(start, tq), h * HEAD_PAD:(h + 1) * HEAD_PAD]
            s = lax.dot_general(qs[h], kj, nt, preferred_element_type=F32)
            if masked:
                r = lax.broadcasted_iota(jnp.int32, (tq, tq), 0)
                c = lax.broadcasted_iota(jnp.int32, (tq, tq), 1)
                s = jnp.where(c <= r, s, NEG)
            m_new = jnp.maximum(m, jnp.max(s, axis=1, keepdims=True))
            alpha = jnp.exp2(m - m_new)
            p = jnp.exp2(s - m_new).astype(BF16)
            acc = alpha * acc + jnp.dot(p, vj, preferred_element_type=F32)
            new.append((m_new, acc))
        return tuple(new)

    init = tuple((jnp.full((tq, 1), NEG, F32), jnp.zeros((tq, HEAD_PAD), F32)) for _ in range(2))
    carry = lax.fori_loop(
        0, i, lambda j, c: step(pl.multiple_of(j * tq, tq), c, False), init)
    carry = step(pl.multiple_of(i * tq, tq), carry, True)

    outs = []
    for h in range(2):
        acc = carry[h][1]
        outs.append(acc / pltpu.roll(acc, 64, 1))
    lane = lax.broadcasted_iota(jnp.int32, (tq, HEAD_PAD), 1)
    o_ref[...] = jnp.where(lane < A_V, outs[0], pltpu.roll(outs[1], 64, 1)).astype(BF16)


def _attn_call(q, k, v, B, S):
    T = B * S
    tq = TQ
    nq = S // tq
    return pl.pallas_call(
        _attn_kernel,
        grid=(B, A_HEADS // 2, nq),
        in_specs=[
            pl.BlockSpec((tq, 2 * HEAD_PAD), lambda b, hp, i: (b * nq + i, hp)),
            pl.BlockSpec((S, 2 * HEAD_PAD), lambda b, hp, i: (b, hp)),
            pl.BlockSpec((S, 2 * HEAD_PAD), lambda b, hp, i: (b, hp)),
        ],
        out_specs=pl.BlockSpec((tq, 2 * A_V), lambda b, hp, i: (b * nq + i, hp)),
        out_shape=jax.ShapeDtypeStruct((T, A_WIDTH), BF16),
        compiler_params=pltpu.CompilerParams(
            dimension_semantics=("arbitrary", "arbitrary", "arbitrary"), vmem_limit_bytes=VMEM_LIMIT),
        name="mla_attn",
    )(q, k, v)


def _ffn_kernel(final_norm, x_ref, hm_ref, ha_ref, wo_ref, fnw_ref, wg_ref, wu_ref, wd_ref, onw_ref, out_ref):
    h = (x_ref[...]
         + jnp.dot(hm_ref[...], wo_ref[0:M_WIDTH, :], preferred_element_type=F32)
         + jnp.dot(ha_ref[...], wo_ref[M_WIDTH:M_WIDTH + A_WIDTH, :], preferred_element_type=F32))
    u = _rms(h, fnw_ref[...]).astype(BF16)
    acc = None
    for c in range(D_FF // FF_CHUNK):
        sl = slice(c * FF_CHUNK, (c + 1) * FF_CHUNK)
        g = jnp.dot(u, wg_ref[:, sl], preferred_element_type=F32)
        up = jnp.dot(u, wu_ref[:, sl], preferred_element_type=F32)
        a = (g * jax.nn.sigmoid(g) * up).astype(BF16)
        d = jnp.dot(a, wd_ref[sl, :], preferred_element_type=F32)
        acc = d if acc is None else acc + d
    y = h + acc
    out_ref[...] = _rms(y, onw_ref[...]) if final_norm else y


def _ffn_call(x2, hm, ha, wo, fnw, wg, wu, wd, onw, final_norm):
    T = x2.shape[0]
    tm = TM_FFN
    const = lambda i: (0, 0)
    tok = lambda i: (i, 0)
    once = pl.Buffered(1)
    return pl.pallas_call(
        functools.partial(_ffn_kernel, final_norm),
        grid=(T // tm,),
        in_specs=[
            pl.BlockSpec((tm, D_MODEL), tok),
            pl.BlockSpec((tm, M_WIDTH), tok),
            pl.BlockSpec((tm, A_WIDTH), tok),
            pl.BlockSpec((M_WIDTH + A_WIDTH, D_MODEL), const, pipeline_mode=once),
            pl.BlockSpec((1, D_MODEL), const),
            pl.BlockSpec((D_MODEL, D_FF), const, pipeline_mode=once),
            pl.BlockSpec((D_MODEL, D_FF), const, pipeline_mode=once),
            pl.BlockSpec((D_FF, D_MODEL), const, pipeline_mode=once),
            pl.BlockSpec((1, D_MODEL), const),
        ],
        out_specs=pl.BlockSpec((tm, D_MODEL), tok),
        out_shape=jax.ShapeDtypeStruct((T, D_MODEL), F32),
        compiler_params=pltpu.CompilerParams(
            dimension_semantics=("arbitrary",), vmem_limit_bytes=VMEM_LIMIT),
        name="ffn",
    )(x2, hm, ha, wo, fnw, wg, wu, wd, onw)


def _prep_w_in(w_in):
    c = 4 * M_WIDTH
    mi = w_in[:, c:c + M_HEADS]
    mf = w_in[:, c + M_HEADS:c + 2 * M_HEADS]
    c += 2 * M_HEADS
    qa = w_in[:, c:c + Q_RANK]
    c += Q_RANK
    kva = w_in[:, c:c + KV_RANK]
    c += KV_RANK
    x1 = w_in[:, c:c + A_ROPE // 2]
    x2 = w_in[:, c + A_ROPE // 2:c + A_ROPE]
    zeros = jnp.zeros((D_MODEL, A_NOPE - 2 * M_HEADS), w_in.dtype)
    misc = jnp.concatenate([mi, mf, zeros, x1, x2, x2, x1], axis=1)
    return jnp.concatenate([w_in[:, 0:4 * M_WIDTH], qa, kva, misc], axis=1).astype(BF16)


def _prep_w_q(w_q_b):
    w = w_q_b.reshape(Q_RANK, A_HEADS, A_NOPE + A_ROPE)
    half = A_ROPE // 2
    nope, x1, x2 = w[:, :, :A_NOPE], w[:, :, A_NOPE:A_NOPE + half], w[:, :, A_NOPE + half:]
    return jnp.concatenate([nope, x1, x2, x2, x1], axis=2).reshape(Q_RANK, A_PAD).astype(BF16)


def _prep_w_kv(w_kv_b):
    w = w_kv_b.reshape(KV_RANK, A_HEADS, A_NOPE + A_V)
    z = jnp.zeros((KV_RANK, A_HEADS, HEAD_PAD - A_NOPE), w.dtype)
    kpart = jnp.concatenate([w[:, :, :A_NOPE], z], axis=2).reshape(KV_RANK, A_PAD)
    vpart = jnp.concatenate([w[:, :, A_NOPE:], z], axis=2).reshape(KV_RANK, A_PAD)
    return jnp.concatenate([kpart, vpart], axis=1).astype(BF16)


def _rope_tables():
    half = A_ROPE // 2
    inv_freq = ROPE_THETA ** (-jnp.arange(half, dtype=F32) / half)
    z16 = jnp.zeros((half,), F32)
    o16 = jnp.ones((half,), F32)
    z32 = jnp.zeros((2 * half,), F32)
    z64 = jnp.zeros((A_NOPE,), F32)
    o64 = jnp.ones((A_NOPE,), F32)
    rows = [
        jnp.concatenate([z64, inv_freq, inv_freq, z32]),
        jnp.concatenate([o64, o16, o16, z32]) * A_SCALE2,
        jnp.concatenate([z64, -o16, o16, z32]) * A_SCALE2,
        jnp.concatenate([z64, o16, o16, z32]),
        jnp.concatenate([z64, -o16, o16, z32]),
        jnp.concatenate([z64, o64]),
        jnp.zeros((LANE,), F32),
        jnp.zeros((LANE,), F32),
    ]
    return jnp.stack(rows)


def kernel(x, positions, attn_norm_w, w_in, b_gates, mlstm_norm_w, q_a_norm_w, w_q_b, kv_a_norm_w, w_kv_b,
           w_out, ffn_norm_w, w_gate, w_up, w_down, final_norm_w):
    B, S, _ = x.shape
    T = B * S
    depth = w_in.shape[0]
    tab = _rope_tables()
    pos2 = positions.reshape(T, 1)
    h = x.reshape(T, D_MODEL)
    for l in range(depth):
        mls, gates, q, k, v = _proj_call(
            h, pos2, attn_norm_w[l].reshape(1, D_MODEL), _prep_w_in(w_in[l]),
            q_a_norm_w[l].reshape(1, Q_RANK), _prep_w_q(w_q_b[l]),
            kv_a_norm_w[l].reshape(1, KV_RANK), _prep_w_kv(w_kv_b[l]), tab)
        bias = jnp.concatenate([b_gates[l], jnp.zeros((MISC - 2 * M_HEADS,), F32)]).reshape(1, MISC)
        hm = _mlstm_call(mls, gates, bias, mlstm_norm_w[l].reshape(1, M_WIDTH), B, S)
        ha = _attn_call(q, k, v, B, S)
        h = _ffn_call(h, hm, ha, w_out[l].astype(BF16), ffn_norm_w[l].reshape(1, D_MODEL),
                      w_gate[l].astype(BF16), w_up[l].astype(BF16), w_down[l].astype(BF16),
                      final_norm_w.reshape(1, D_MODEL), final_norm=(l == depth - 1))
    return h.reshape(B, S, D_MODEL)
```

```python
import functools
import math

import jax
import jax.numpy as jnp
from jax import lax
from jax.experimental import pallas as pl
from jax.experimental.pallas import tpu as pltpu

F32 = jnp.float32
BF16 = jnp.bfloat16

D_MODEL = 1024
M_HEADS = 4
M_HEAD_DIM = 128
M_WIDTH = M_HEADS * M_HEAD_DIM
A_HEADS = 8
A_NOPE = 64
A_ROPE = 32
A_V = 64
A_WIDTH = A_HEADS * A_V
Q_RANK = 384
KV_RANK = 256
ROPE_THETA = 10000.0
D_FF = 2816
EPS = 1e-6

LANE = 128
HEAD_PAD = LANE
A_PAD = A_HEADS * HEAD_PAD
MISC = LANE
D_IN_PAD = 4 * M_WIDTH + Q_RANK + KV_RANK + MISC
VMEM_LIMIT = 56 * 1024 * 1024

TM_PROJ = 512
TM_FFN = 512
M_CHUNK = 256
TQ = 512
A_GROUP = 4
FF_CHUNK = 256

M_SCALE = M_HEAD_DIM ** -0.5
A_SCALE2 = (A_NOPE + A_ROPE) ** -0.5 * math.log2(math.e)
NEG = -1e30


def _rms(x, w):
    return x * lax.rsqrt(jnp.mean(x * x, axis=-1, keepdims=True) + EPS) * w


def _proj_kernel(x_ref, pos_ref, nw_ref, w1_ref, qnw_ref, wq_ref, kvnw_ref, wkv_ref, tab_ref,
                 mqvo_ref, kt_ref, g_ref, q_ref, k_ref, v_ref):
    x = x_ref[...]
    u = _rms(x, nw_ref[...]).astype(BF16)

    pa = jnp.dot(u, w1_ref[:, 0:4 * M_WIDTH], preferred_element_type=F32)
    mqvo_ref[:, 0:M_WIDTH] = (pa[:, 0:M_WIDTH] * M_SCALE).astype(BF16)
    mqvo_ref[:, M_WIDTH:3 * M_WIDTH] = pa[:, M_WIDTH:3 * M_WIDTH].astype(BF16)
    kt_ref[...] = pa[:, 3 * M_WIDTH:4 * M_WIDTH].T.astype(BF16)

    pb = jnp.dot(u, w1_ref[:, 4 * M_WIDTH:D_IN_PAD], preferred_element_type=F32)
    qa = pb[:, 0:Q_RANK]
    kva = pb[:, Q_RANK:Q_RANK + KV_RANK]
    misc = pb[:, Q_RANK + KV_RANK:Q_RANK + KV_RANK + MISC]
    g_ref[...] = misc.T[0:2 * M_HEADS, :]

    inv_freq = tab_ref[0:1, :]
    ang = pos_ref[...].astype(F32) * inv_freq
    cos = jnp.cos(ang)
    sin = jnp.sin(ang)
    qa_tab = cos * tab_ref[1:2, :]
    qb_tab = sin * tab_ref[2:3, :]
    ka_tab = cos * tab_ref[3:4, :]
    kb_tab = sin * tab_ref[4:5, :]

    kr = misc * ka_tab + pltpu.roll(misc, 96, 1) * kb_tab

    cq = jnp.dot(_rms(qa, qnw_ref[...]).astype(BF16), wq_ref[...], preferred_element_type=F32)
    ckv = jnp.dot(_rms(kva, kvnw_ref[...]).astype(BF16), wkv_ref[...], preferred_element_type=F32)
    v_one = tab_ref[5:6, :]
    for h in range(A_HEADS):
        sl = slice(h * HEAD_PAD, (h + 1) * HEAD_PAD)
        qh = cq[:, sl]
        q_ref[:, sl] = (qh * qa_tab + pltpu.roll(qh, 96, 1) * qb_tab).astype(BF16)
        k_ref[:, sl] = (ckv[:, sl] + kr).astype(BF16)
        v_ref[:, sl] = (ckv[:, A_PAD + h * HEAD_PAD:A_PAD + (h + 1) * HEAD_PAD] + v_one).astype(BF16)


def _proj_call(x2, pos2, nw, w1, qnw, wq, kvnw, wkv, tab):
    T = x2.shape[0]
    tm = TM_PROJ
    const = lambda i: (0, 0)
    tok = lambda i: (i, 0)
    tokt = lambda i: (0, i)
    return pl.pallas_call(
        _proj_kernel,
        grid=(T // tm,),
        in_specs=[
            pl.BlockSpec((tm, D_MODEL), tok),
            pl.BlockSpec((tm, 1), tok),
            pl.BlockSpec((1, D_MODEL), const),
            pl.BlockSpec((D_MODEL, D_IN_PAD), const),
            pl.BlockSpec((1, Q_RANK), const),
            pl.BlockSpec((Q_RANK, A_PAD), const),
            pl.BlockSpec((1, KV_RANK), const),
            pl.BlockSpec((KV_RANK, 2 * A_PAD), const),
            pl.BlockSpec((8, LANE), const),
        ],
        out_specs=[
            pl.BlockSpec((tm, 3 * M_WIDTH), tok),
            pl.BlockSpec((M_WIDTH, tm), tokt),
            pl.BlockSpec((2 * M_HEADS, tm), tokt),
            pl.BlockSpec((tm, A_PAD), tok),
            pl.BlockSpec((tm, A_PAD), tok),
            pl.BlockSpec((tm, A_PAD), tok),
        ],
        out_shape=[
            jax.ShapeDtypeStruct((T, 3 * M_WIDTH), BF16),
            jax.ShapeDtypeStruct((M_WIDTH, T), BF16),
            jax.ShapeDtypeStruct((2 * M_HEADS, T), F32),
            jax.ShapeDtypeStruct((T, A_PAD), BF16),
            jax.ShapeDtypeStruct((T, A_PAD), BF16),
            jax.ShapeDtypeStruct((T, A_PAD), BF16),
        ],
        compiler_params=pltpu.CompilerParams(
            dimension_semantics=("arbitrary",), vmem_limit_bytes=VMEM_LIMIT),
        name="proj",
    )(x2, pos2, nw, w1, qnw, wq, kvnw, wkv, tab)


def _mlstm_kernel(q_ref, v_ref, o_ref, kt_ref, g_ref, bias_ref, nw_ref, out_ref, c_ref, m_ref):
    L = M_CHUNK

    @pl.when(pl.program_id(1) == 0)
    def _():
        c_ref[...] = jnp.zeros_like(c_ref)
        m_ref[...] = jnp.zeros_like(m_ref)

    x = g_ref[...] + bias_ref[...]
    lf = jax.nn.log_sigmoid(x)
    row = lax.broadcasted_iota(jnp.int32, (L, L), 0)
    col = lax.broadcasted_iota(jnp.int32, (L, L), 1)
    causal = col <= row
    p1 = lf.astype(BF16)
    r1 = lf - p1.astype(F32)
    p2 = r1.astype(BF16)
    p3 = (r1 - p2.astype(F32)).astype(BF16)
    triu = (row <= col).astype(BF16)
    cum = jnp.dot(jnp.concatenate([p1, p2, p3], axis=0), triu, preferred_element_type=F32)
    b_rows = cum[0:8] + cum[8:16] + cum[16:24]
    ones = jnp.ones((L, M_HEAD_DIM), BF16)

    for h in range(M_HEADS):
        sl = slice(h * M_HEAD_DIM, (h + 1) * M_HEAD_DIM)
        b_row = b_rows[M_HEADS + h:M_HEADS + h + 1, :]
        lf_row = lf[M_HEADS + h:M_HEADS + h + 1, :]
        c_row = x[h:h + 1, :] - b_row
        g_tot = b_row[:, L - 1:L]
        m_prev = m_ref[h][0:1, 0:1]

        qh = q_ref[:, sl]
        kt = kt_ref[sl, :]
        vx = jnp.concatenate([v_ref[:, sl], ones], axis=1)

        a_row = g_tot + c_row
        m_loc = jnp.max(a_row, axis=1, keepdims=True)
        w_row = jnp.exp(a_row - m_loc)
        c_loc = jnp.dot((kt.astype(F32) * w_row).astype(BF16), vx, preferred_element_type=F32)

        cmat = jnp.where(causal, c_row, NEG)
        mm_t = jnp.maximum(jnp.max(cmat, axis=1, keepdims=True), m_prev)
        b_col = jnp.sum(jnp.where(causal, lf_row, 0.0), axis=1, keepdims=True)
        s = jnp.dot(qh, kt, preferred_element_type=F32)
        p = jnp.exp(cmat - mm_t) * s
        s_inter = jnp.exp(m_prev - mm_t)
        c_prev = c_ref[h]
        pv = jnp.dot(p.astype(BF16), vx, preferred_element_type=F32)
        qc = jnp.dot(qh, c_prev.astype(BF16), preferred_element_type=F32)
        num = pv[:, 0:M_HEAD_DIM] + s_inter * qc[:, 0:M_HEAD_DIM]
        den = pv[:, M_HEAD_DIM:M_HEAD_DIM + 1] + s_inter * qc[:, M_HEAD_DIM:M_HEAD_DIM + 1]
        den = jnp.maximum(jnp.abs(den), jnp.exp(-(b_col + mm_t)))
        hh = num * (1.0 / den)
        hn = _rms(hh, nw_ref[:, sl])
        out_ref[:, sl] = (hn * jax.nn.sigmoid(o_ref[:, sl].astype(F32))).astype(BF16)

        m_new = jnp.maximum(g_tot + m_prev, m_loc)
        s_old = jnp.exp(g_tot + m_prev - m_new)
        s_loc = jnp.exp(m_loc - m_new)
        c_ref[h] = s_old * c_prev + s_loc * c_loc
        m_ref[h] = jnp.broadcast_to(m_new, (8, LANE))


def _mlstm_call(mqvo, kt, gt, bias, nw, B, S):
    T = B * S
    L = M_CHUNK
    nc = S // L
    blk = lambda j: (lambda b, c: (b * nc + c, j))
    blkt = lambda b, c: (0, b * nc + c)
    const = lambda b, c: (0, 0)
    return pl.pallas_call(
        _mlstm_kernel,
        grid=(B, nc),
        in_specs=[
            pl.BlockSpec((L, M_WIDTH), blk(0)),
            pl.BlockSpec((L, M_WIDTH), blk(1)),
            pl.BlockSpec((L, M_WIDTH), blk(2)),
            pl.BlockSpec((M_WIDTH, L), blkt),
            pl.BlockSpec((2 * M_HEADS, L), blkt),
            pl.BlockSpec((2 * M_HEADS, 1), const),
            pl.BlockSpec((1, M_WIDTH), const),
        ],
        out_specs=pl.BlockSpec((L, M_WIDTH), blk(0)),
        out_shape=jax.ShapeDtypeStruct((T, M_WIDTH), BF16),
        scratch_shapes=[
            pltpu.VMEM((M_HEADS, M_HEAD_DIM, 2 * M_HEAD_DIM), F32),
            pltpu.VMEM((M_HEADS, 8, LANE), F32),
        ],
        compiler_params=pltpu.CompilerParams(
            dimension_semantics=("arbitrary", "arbitrary"), vmem_limit_bytes=VMEM_LIMIT),
        name="mlstm",
    )(mqvo, mqvo, mqvo, kt, gt, bias, nw)


def _attn_kernel(q_ref, k_ref, v_ref, o_ref):
    i = pl.program_id(2)
    tq = TQ
    nt = (((1,), (1,)), ((), ()))
    heads = range(A_GROUP)
    qs = [q_ref[:, h * HEAD_PAD:(h + 1) * HEAD_PAD] for h in heads]

    def step(start, carry, masked):
        new = []
        for h in heads:
            m, acc = carry[h]
            kj = k_ref[pl.ds(start, tq), h * HEAD_PAD:(h + 1) * HEAD_PAD]
            vj = v_ref[pl.ds(start, tq), h * HEAD_PAD:(h + 1) * HEAD_PAD]
            s = lax.dot_general(qs[h], kj, nt, preferred_element_type=F32)
            if masked:
                r = lax.broadcasted_iota(jnp.int32, (tq, tq), 0)
                c = lax.broadcasted_iota(jnp.int32, (tq, tq), 1)
                s = jnp.where(c <= r, s, NEG)
            m_new = jnp.maximum(m, jnp.max(s, axis=1, keepdims=True))
            alpha = jnp.exp2(m - m_new)
            p = jnp.exp2(s - m_new).astype(BF16)
            acc = alpha * acc + jnp.dot(p, vj, preferred_element_type=F32)
            new.append((m_new, acc))
        return tuple(new)

    init = tuple((jnp.full((tq, 1), NEG, F32), jnp.zeros((tq, HEAD_PAD), F32)) for _ in heads)
    carry = lax.fori_loop(
        0, i, lambda j, c: step(pl.multiple_of(j * tq, tq), c, False), init)
    carry = step(pl.multiple_of(i * tq, tq), carry, True)

    lane = lax.broadcasted_iota(jnp.int32, (tq, HEAD_PAD), 1)
    for hp in range(A_GROUP // 2):
        o = [carry[2 * hp + e][1] for e in range(2)]
        o = [a / pltpu.roll(a, 64, 1) for a in o]
        o_ref[:, hp * HEAD_PAD:(hp + 1) * HEAD_PAD] = jnp.where(
            lane < A_V, o[0], pltpu.roll(o[1], 64, 1)).astype(BF16)


def _attn_call(q, k, v, B, S):
    T = B * S
    tq = TQ
    nq = S // tq
    gw = A_GROUP * HEAD_PAD
    return pl.pallas_call(
        _attn_kernel,
        grid=(B, A_HEADS // A_GROUP, nq),
        in_specs=[
            pl.BlockSpec((tq, gw), lambda b, g, i: (b * nq + i, g)),
            pl.BlockSpec((S, gw), lambda b, g, i: (b, g)),
            pl.BlockSpec((S, gw), lambda b, g, i: (b, g)),
        ],
        out_specs=pl.BlockSpec((tq, A_GROUP * A_V), lambda b, g, i: (b * nq + i, g)),
        out_shape=jax.ShapeDtypeStruct((T, A_WIDTH), BF16),
        compiler_params=pltpu.CompilerParams(
            dimension_semantics=("arbitrary", "arbitrary", "arbitrary"), vmem_limit_bytes=VMEM_LIMIT),
        name="mla_attn",
    )(q, k, v)


def _ffn_kernel(final_norm, x_ref, hm_ref, ha_ref, wo_ref, fnw_ref, wg_ref, wu_ref, wd_ref, onw_ref, out_ref):
    h = (x_ref[...]
         + jnp.dot(hm_ref[...], wo_ref[0:M_WIDTH, :], preferred_element_type=F32)
         + jnp.dot(ha_ref[...], wo_ref[M_WIDTH:M_WIDTH + A_WIDTH, :], preferred_element_type=F32))
    u = _rms(h, fnw_ref[...]).astype(BF16)
    acc = None
    for c in range(D_FF // FF_CHUNK):
        sl = slice(c * FF_CHUNK, (c + 1) * FF_CHUNK)
        g = jnp.dot(u, wg_ref[:, sl], preferred_element_type=F32)
        up = jnp.dot(u, wu_ref[:, sl], preferred_element_type=F32)
        a = (g * jax.nn.sigmoid(g) * up).astype(BF16)
        d = jnp.dot(a, wd_ref[sl, :], preferred_element_type=F32)
        acc = d if acc is None else acc + d
    y = h + acc
    out_ref[...] = _rms(y, onw_ref[...]) if final_norm else y


def _ffn_call(x2, hm, ha, wo, fnw, wg, wu, wd, onw, final_norm):
    T = x2.shape[0]
    tm = TM_FFN
    const = lambda i: (0, 0)
    tok = lambda i: (i, 0)
    once = pl.Buffered(1)
    return pl.pallas_call(
        functools.partial(_ffn_kernel, final_norm),
        grid=(T // tm,),
        in_specs=[
            pl.BlockSpec((tm, D_MODEL), tok),
            pl.BlockSpec((tm, M_WIDTH), tok),
            pl.BlockSpec((tm, A_WIDTH), tok),
            pl.BlockSpec((M_WIDTH + A_WIDTH, D_MODEL), const, pipeline_mode=once),
            pl.BlockSpec((1, D_MODEL), const),
            pl.BlockSpec((D_MODEL, D_FF), const, pipeline_mode=once),
            pl.BlockSpec((D_MODEL, D_FF), const, pipeline_mode=once),
            pl.BlockSpec((D_FF, D_MODEL), const, pipeline_mode=once),
            pl.BlockSpec((1, D_MODEL), const),
        ],
        out_specs=pl.BlockSpec((tm, D_MODEL), tok),
        out_shape=jax.ShapeDtypeStruct((T, D_MODEL), F32),
        compiler_params=pltpu.CompilerParams(
            dimension_semantics=("arbitrary",), vmem_limit_bytes=VMEM_LIMIT),
        name="ffn",
    )(x2, hm, ha, wo, fnw, wg, wu, wd, onw)


def _prep_w_in(w_in):
    c = 4 * M_WIDTH
    mi = w_in[:, c:c + M_HEADS]
    mf = w_in[:, c + M_HEADS:c + 2 * M_HEADS]
    c += 2 * M_HEADS
    qa = w_in[:, c:c + Q_RANK]
    c += Q_RANK
    kva = w_in[:, c:c + KV_RANK]
    c += KV_RANK
    x1 = w_in[:, c:c + A_ROPE // 2]
    x2 = w_in[:, c + A_ROPE // 2:c + A_ROPE]
    zeros = jnp.zeros((D_MODEL, A_NOPE - 2 * M_HEADS), w_in.dtype)
    misc = jnp.concatenate([mi, mf, zeros, x1, x2, x2, x1], axis=1)
    mq, mk, mv, mo = (w_in[:, j * M_WIDTH:(j + 1) * M_WIDTH] for j in range(4))
    return jnp.concatenate([mq, mv, mo, mk, qa, kva, misc], axis=1).astype(BF16)


def _prep_w_q(w_q_b):
    w = w_q_b.reshape(Q_RANK, A_HEADS, A_NOPE + A_ROPE)
    half = A_ROPE // 2
    nope, x1, x2 = w[:, :, :A_NOPE], w[:, :, A_NOPE:A_NOPE + half], w[:, :, A_NOPE + half:]
    return jnp.concatenate([nope, x1, x2, x2, x1], axis=2).reshape(Q_RANK, A_PAD).astype(BF16)


def _prep_w_kv(w_kv_b):
    w = w_kv_b.reshape(KV_RANK, A_HEADS, A_NOPE + A_V)
    z = jnp.zeros((KV_RANK, A_HEADS, HEAD_PAD - A_NOPE), w.dtype)
    kpart = jnp.concatenate([w[:, :, :A_NOPE], z], axis=2).reshape(KV_RANK, A_PAD)
    vpart = jnp.concatenate([w[:, :, A_NOPE:], z], axis=2).reshape(KV_RANK, A_PAD)
    return jnp.concatenate([kpart, vpart], axis=1).astype(BF16)


def _rope_tables():
    half = A_ROPE // 2
    inv_freq = ROPE_THETA ** (-jnp.arange(half, dtype=F32) / half)
    z16 = jnp.zeros((half,), F32)
    o16 = jnp.ones((half,), F32)
    z32 = jnp.zeros((2 * half,), F32)
    z64 = jnp.zeros((A_NOPE,), F32)
    o64 = jnp.ones((A_NOPE,), F32)
    rows = [
        jnp.concatenate([z64, inv_freq, inv_freq, z32]),
        jnp.concatenate([o64, o16, o16, z32]) * A_SCALE2,
        jnp.concatenate([z64, -o16, o16, z32]) * A_SCALE2,
        jnp.concatenate([z64, o16, o16, z32]),
        jnp.concatenate([z64, -o16, o16, z32]),
        jnp.concatenate([z64, o64]),
        jnp.zeros((LANE,), F32),
        jnp.zeros((LANE,), F32),
    ]
    return jnp.stack(rows)


def kernel(x, positions, attn_norm_w, w_in, b_gates, mlstm_norm_w, q_a_norm_w, w_q_b, kv_a_norm_w, w_kv_b,
           w_out, ffn_norm_w, w_gate, w_up, w_down, final_norm_w):
    B, S, _ = x.shape
    T = B * S
    depth = w_in.shape[0]
    tab = _rope_tables()
    pos2 = positions.reshape(T, 1)
    h = x.reshape(T, D_MODEL)
    for l in range(depth):
        mqvo, kt, gt, q, k, v = _proj_call(
            h, pos2, attn_norm_w[l].reshape(1, D_MODEL), _prep_w_in(w_in[l]),
            q_a_norm_w[l].reshape(1, Q_RANK), _prep_w_q(w_q_b[l]),
            kv_a_norm_w[l].reshape(1, KV_RANK), _prep_w_kv(w_kv_b[l]), tab)
        hm = _mlstm_call(mqvo, kt, gt, b_gates[l].reshape(2 * M_HEADS, 1),
                         mlstm_norm_w[l].reshape(1, M_WIDTH), B, S)
        ha = _attn_call(q, k, v, B, S)
        h = _ffn_call(h, hm, ha, w_out[l].astype(BF16), ffn_norm_w[l].reshape(1, D_MODEL),
                      w_gate[l].astype(BF16), w_up[l].astype(BF16), w_down[l].astype(BF16),
                      final_norm_w.reshape(1, D_MODEL), final_norm=(l == depth - 1))
    return h.reshape(B, S, D_MODEL)
```

```python
import functools
import math

import jax
import jax.numpy as jnp
from jax import lax
from jax.experimental import pallas as pl
from jax.experimental.pallas import tpu as pltpu

F32 = jnp.float32
BF16 = jnp.bfloat16

D_MODEL = 1024
M_HEADS = 4
M_HEAD_DIM = 128
M_WIDTH = M_HEADS * M_HEAD_DIM
A_HEADS = 8
A_NOPE = 64
A_ROPE = 32
A_V = 64
A_WIDTH = A_HEADS * A_V
Q_RANK = 384
KV_RANK = 256
ROPE_THETA = 10000.0
D_FF = 2816
EPS = 1e-6

LANE = 128
HEAD_PAD = LANE
A_PAD = A_HEADS * HEAD_PAD
MISC = LANE
D_LAT = Q_RANK + KV_RANK + MISC
VMEM_LIMIT = 56 * 1024 * 1024

TM_PROJ = 512
TM_FFN = 512
M_CHUNK = 256
TQ = 512
A_GROUP = 4
FF_CHUNK = 256

M_SCALE = M_HEAD_DIM ** -0.5
A_SCALE2 = (A_NOPE + A_ROPE) ** -0.5 * math.log2(math.e)
NEG = -1e30


def _rms(x, w):
    return x * lax.rsqrt(jnp.mean(x * x, axis=-1, keepdims=True) + EPS) * w


def _proj_kernel(x_ref, pos_ref, nw_ref, wmq_ref, wmv_ref, wmo_ref, wmk_ref, wlat_ref, qnw_ref, wq_ref, kvnw_ref,
                 wk_ref, wvt_ref, vone_ref, invf_ref, kmask_ref, mqvo_ref, kt_ref, g_ref, q_ref, k_ref, vt_ref):
    tm = x_ref.shape[0]
    x = x_ref[...]
    u = _rms(x, nw_ref[...]).astype(BF16)

    def proj(w_ref):
        return jnp.dot(u, w_ref[...], preferred_element_type=F32)

    pb = proj(wlat_ref)
    qa = pb[:, 0:Q_RANK]
    kva = pb[:, Q_RANK:Q_RANK + KV_RANK]
    misc = pb[:, Q_RANK + KV_RANK:Q_RANK + KV_RANK + MISC]

    ang = invf_ref[...] * pos_ref[0].astype(F32)
    cos = jnp.cos(ang)
    sin = jnp.sin(ang)
    half = A_ROPE // 2
    one = jnp.ones((A_NOPE, tm), F32)
    z_lo = jnp.zeros((A_NOPE, tm), F32)
    z_hi = jnp.zeros((HEAD_PAD - A_NOPE - 2 * half, tm), F32)
    a_tab = jnp.concatenate([one, cos, cos, z_hi], axis=0).T
    b_tab = jnp.concatenate([z_lo, -sin, sin, z_hi], axis=0).T

    pq = proj(wmq_ref)

    qn = _rms(qa, qnw_ref[...] * A_SCALE2).astype(BF16)
    kvn = _rms(kva, kvnw_ref[...])
    cq = jnp.dot(qn, wq_ref[...], preferred_element_type=F32)
    ck = jnp.dot(kvn.astype(BF16), wk_ref[...], preferred_element_type=F32)
    vt = jnp.dot(wvt_ref[...], kvn.T.astype(BF16), preferred_element_type=F32)
    mqvo_ref[:, 0:M_WIDTH] = (pq * M_SCALE).astype(BF16)

    pv = proj(wmv_ref)
    for h in range(A_HEADS):
        sl = slice(h * HEAD_PAD, (h + 1) * HEAD_PAD)
        qh = cq[:, sl]
        q_ref[:, sl] = (qh * a_tab + pltpu.roll(qh, 96, 1) * b_tab).astype(BF16)

    po = proj(wmo_ref)
    g_ref[...] = misc.T[0:2 * M_HEADS, :]
    krin = misc * kmask_ref[...]
    kr = krin * a_tab + pltpu.roll(krin, 96, 1) * b_tab
    for h in range(A_HEADS):
        sl = slice(h * HEAD_PAD, (h + 1) * HEAD_PAD)
        k_ref[:, sl] = (ck[:, sl] + kr).astype(BF16)
    vt_ref[0] = (vt + vone_ref[...]).astype(BF16)

    pk = proj(wmk_ref)
    mqvo_ref[:, M_WIDTH:2 * M_WIDTH] = pv.astype(BF16)
    mqvo_ref[:, 2 * M_WIDTH:3 * M_WIDTH] = po.astype(BF16)
    kt_ref[...] = pk.T.astype(BF16)


def _proj_call(x2, pos3, nw, wm, wlat, qnw, wq, kvnw, wk, wvt, vone, invf, kmask):
    T = x2.shape[0]
    tm = TM_PROJ
    assert tm == TQ, "the transposed value blocks are written one attention key block per projection tile"
    const = lambda i: (0, 0)
    tok = lambda i: (i, 0)
    tokt = lambda i: (0, i)
    return pl.pallas_call(
        _proj_kernel,
        grid=(T // tm,),
        in_specs=[
            pl.BlockSpec((tm, D_MODEL), tok),
            pl.BlockSpec((1, 1, tm), lambda i: (i, 0, 0)),
            pl.BlockSpec((1, D_MODEL), const),
            pl.BlockSpec((D_MODEL, M_WIDTH), const),
            pl.BlockSpec((D_MODEL, M_WIDTH), const),
            pl.BlockSpec((D_MODEL, M_WIDTH), const),
            pl.BlockSpec((D_MODEL, M_WIDTH), const),
            pl.BlockSpec((D_MODEL, D_LAT), const),
            pl.BlockSpec((1, Q_RANK), const),
            pl.BlockSpec((Q_RANK, A_PAD), const),
            pl.BlockSpec((1, KV_RANK), const),
            pl.BlockSpec((KV_RANK, A_PAD), const),
            pl.BlockSpec((A_PAD, KV_RANK), const),
            pl.BlockSpec((A_PAD, 1), const),
            pl.BlockSpec((A_ROPE // 2, 1), const),
            pl.BlockSpec((1, HEAD_PAD), const),
        ],
        out_specs=[
            pl.BlockSpec((tm, 3 * M_WIDTH), tok),
            pl.BlockSpec((M_WIDTH, tm), tokt),
            pl.BlockSpec((2 * M_HEADS, tm), tokt),
            pl.BlockSpec((tm, A_PAD), tok),
            pl.BlockSpec((tm, A_PAD), tok),
            pl.BlockSpec((1, A_PAD, tm), lambda i: (i, 0, 0)),
        ],
        out_shape=[
            jax.ShapeDtypeStruct((T, 3 * M_WIDTH), BF16),
            jax.ShapeDtypeStruct((M_WIDTH, T), BF16),
            jax.ShapeDtypeStruct((2 * M_HEADS, T), F32),
            jax.ShapeDtypeStruct((T, A_PAD), BF16),
            jax.ShapeDtypeStruct((T, A_PAD), BF16),
            jax.ShapeDtypeStruct((T // tm, A_PAD, tm), BF16),
        ],
        compiler_params=pltpu.CompilerParams(
            dimension_semantics=("arbitrary",), vmem_limit_bytes=VMEM_LIMIT),
        name="proj",
    )(x2, pos3, nw, *wm, wlat, qnw, wq, kvnw, wk, wvt, vone, invf, kmask)


def _mlstm_kernel(q_ref, v_ref, o_ref, kt_ref, g_ref, bias_ref, nw_ref, out_ref, c_ref, m_ref):
    L = M_CHUNK

    @pl.when(pl.program_id(1) == 0)
    def _():
        c_ref[...] = jnp.zeros_like(c_ref)
        m_ref[...] = jnp.zeros_like(m_ref)

    x = g_ref[...] + bias_ref[...]
    lf = jax.nn.log_sigmoid(x)
    row = lax.broadcasted_iota(jnp.int32, (L, L), 0)
    col = lax.broadcasted_iota(jnp.int32, (L, L), 1)
    causal = col <= row
    p1 = lf.astype(BF16)
    r1 = lf - p1.astype(F32)
    p2 = r1.astype(BF16)
    p3 = (r1 - p2.astype(F32)).astype(BF16)
    triu = (row <= col).astype(BF16)
    cum = jnp.dot(jnp.concatenate([p1, p2, p3], axis=0), triu, preferred_element_type=F32)
    b_rows = cum[0:8] + cum[8:16] + cum[16:24]
    ones = jnp.ones((L, M_HEAD_DIM), BF16)

    for h in range(M_HEADS):
        sl = slice(h * M_HEAD_DIM, (h + 1) * M_HEAD_DIM)
        b_row = b_rows[M_HEADS + h:M_HEADS + h + 1, :]
        lf_row = lf[M_HEADS + h:M_HEADS + h + 1, :]
        c_row = x[h:h + 1, :] - b_row
        g_tot = b_row[:, L - 1:L]
        m_prev = m_ref[h][0:1, 0:1]

        qh = q_ref[:, sl]
        kt = kt_ref[sl, :]
        vx = jnp.concatenate([v_ref[:, sl], ones], axis=1)

        a_row = g_tot + c_row
        m_loc = jnp.max(a_row, axis=1, keepdims=True)
        w_row = jnp.exp(a_row - m_loc)
        c_loc = jnp.dot((kt.astype(F32) * w_row).astype(BF16), vx, preferred_element_type=F32)

        cmat = jnp.where(causal, c_row, NEG)
        mm_t = jnp.maximum(jnp.max(cmat, axis=1, keepdims=True), m_prev)
        b_col = jnp.sum(jnp.where(causal, lf_row, 0.0), axis=1, keepdims=True)
        s = jnp.dot(qh, kt, preferred_element_type=F32)
        p = jnp.exp(cmat - mm_t) * s
        s_inter = jnp.exp(m_prev - mm_t)
        c_prev = c_ref[h]
        pv = jnp.dot(p.astype(BF16), vx, preferred_element_type=F32)
        qc = jnp.dot(qh, c_prev.astype(BF16), preferred_element_type=F32)
        num = pv[:, 0:M_HEAD_DIM] + s_inter * qc[:, 0:M_HEAD_DIM]
        den = pv[:, M_HEAD_DIM:M_HEAD_DIM + 1] + s_inter * qc[:, M_HEAD_DIM:M_HEAD_DIM + 1]
        den = jnp.maximum(jnp.abs(den), jnp.exp(-(b_col + mm_t)))
        hh = num * (1.0 / den)
        hn = _rms(hh, nw_ref[:, sl])
        out_ref[:, sl] = (hn * jax.nn.sigmoid(o_ref[:, sl].astype(F32))).astype(BF16)

        m_new = jnp.maximum(g_tot + m_prev, m_loc)
        s_old = jnp.exp(g_tot + m_prev - m_new)
        s_loc = jnp.exp(m_loc - m_new)
        c_ref[h] = s_old * c_prev + s_loc * c_loc
        m_ref[h] = jnp.broadcast_to(m_new, (8, LANE))


def _mlstm_call(mqvo, kt, gt, bias, nw, B, S):
    T = B * S
    L = M_CHUNK
    nc = S // L
    blk = lambda j: (lambda b, c: (b * nc + c, j))
    blkt = lambda b, c: (0, b * nc + c)
    const = lambda b, c: (0, 0)
    return pl.pallas_call(
        _mlstm_kernel,
        grid=(B, nc),
        in_specs=[
            pl.BlockSpec((L, M_WIDTH), blk(0)),
            pl.BlockSpec((L, M_WIDTH), blk(1)),
            pl.BlockSpec((L, M_WIDTH), blk(2)),
            pl.BlockSpec((M_WIDTH, L), blkt),
            pl.BlockSpec((2 * M_HEADS, L), blkt),
            pl.BlockSpec((2 * M_HEADS, 1), const),
            pl.BlockSpec((1, M_WIDTH), const),
        ],
        out_specs=pl.BlockSpec((L, M_WIDTH), blk(0)),
        out_shape=jax.ShapeDtypeStruct((T, M_WIDTH), BF16),
        scratch_shapes=[
            pltpu.VMEM((M_HEADS, M_HEAD_DIM, 2 * M_HEAD_DIM), F32),
            pltpu.VMEM((M_HEADS, 8, LANE), F32),
        ],
        compiler_params=pltpu.CompilerParams(
            dimension_semantics=("arbitrary", "arbitrary"), vmem_limit_bytes=VMEM_LIMIT),
        name="mlstm",
    )(mqvo, mqvo, mqvo, kt, gt, bias, nw)


def _attn_kernel(q_ref, k_ref, vt_ref, o_ref):
    i = pl.program_id(2)
    tq = TQ
    nt = (((1,), (1,)), ((), ()))
    heads = range(A_GROUP)
    qs = [q_ref[:, h * HEAD_PAD:(h + 1) * HEAD_PAD] for h in heads]

    def step(blocks, carry, masked):
        sts = []
        for h in heads:
            for j in blocks:
                start = pl.multiple_of(j * tq, tq)
                kj = k_ref[pl.ds(start, tq), h * HEAD_PAD:(h + 1) * HEAD_PAD]
                st = lax.dot_general(kj, qs[h], nt, preferred_element_type=F32)
                if masked:
                    key = lax.broadcasted_iota(jnp.int32, (tq, tq), 0)
                    qry = lax.broadcasted_iota(jnp.int32, (tq, tq), 1)
                    st = jnp.where(key <= qry, st, NEG)
                sts.append(st)
        nb = len(blocks)
        soft = []
        for h in heads:
            m = carry[h][0]
            m_new = m
            for st in sts[h * nb:(h + 1) * nb]:
                m_new = jnp.maximum(m_new, jnp.max(st, axis=0, keepdims=True))
            pts = [jnp.exp2(st - m_new).astype(BF16) for st in sts[h * nb:(h + 1) * nb]]
            soft.append((m_new, jnp.exp2(m - m_new), pts))
        new = []
        for h in heads:
            m_new, alpha, pts = soft[h]
            pv = None
            for j, pt in zip(blocks, pts):
                vtj = vt_ref[j, h * HEAD_PAD:(h + 1) * HEAD_PAD, :]
                d = jnp.dot(vtj, pt, preferred_element_type=F32)
                pv = d if pv is None else pv + d
            new.append((m_new, alpha * carry[h][1] + pv))
        return tuple(new)

    init = tuple((jnp.full((1, tq), NEG, F32), jnp.zeros((HEAD_PAD, tq), F32)) for _ in heads)
    carry = lax.fori_loop(0, i, lambda j, c: step((j,), c, False), init)
    carry = step((i,), carry, True)

    for hp in range(A_GROUP // 2):
        ot = [carry[2 * hp + e][1] for e in range(2)]
        ot = [a[0:A_V, :] * (1.0 / a[A_V:A_V + 1, :]) for a in ot]
        o_ref[:, hp * HEAD_PAD:(hp + 1) * HEAD_PAD] = jnp.concatenate(ot, axis=0).T.astype(BF16)


def _attn_call(q, k, vt, B, S):
    T = B * S
    tq = TQ
    nq = S // tq
    gw = A_GROUP * HEAD_PAD
    return pl.pallas_call(
        _attn_kernel,
        grid=(B, A_HEADS // A_GROUP, nq),
        in_specs=[
            pl.BlockSpec((tq, gw), lambda b, g, i: (b * nq + i, g)),
            pl.BlockSpec((S, gw), lambda b, g, i: (b, g)),
            pl.BlockSpec((nq, gw, tq), lambda b, g, i: (b, g, 0)),
        ],
        out_specs=pl.BlockSpec((tq, A_GROUP * A_V), lambda b, g, i: (b * nq + i, g)),
        out_shape=jax.ShapeDtypeStruct((T, A_WIDTH), BF16),
        compiler_params=pltpu.CompilerParams(
            dimension_semantics=("arbitrary", "arbitrary", "arbitrary"), vmem_limit_bytes=VMEM_LIMIT),
        name="mla_attn",
    )(q, k, vt)


def _ffn_kernel(final_norm, x_ref, hm_ref, ha_ref, wo_ref, fnw_ref, wg_ref, wu_ref, wd_ref, onw_ref, out_ref):
    h = (x_ref[...]
         + jnp.dot(hm_ref[...], wo_ref[0:M_WIDTH, :], preferred_element_type=F32)
         + jnp.dot(ha_ref[...], wo_ref[M_WIDTH:M_WIDTH + A_WIDTH, :], preferred_element_type=F32))
    u = _rms(h, fnw_ref[...]).astype(BF16)
    acc = None
    for c in range(D_FF // FF_CHUNK):
        sl = slice(c * FF_CHUNK, (c + 1) * FF_CHUNK)
        g = jnp.dot(u, wg_ref[:, sl], preferred_element_type=F32)
        up = jnp.dot(u, wu_ref[:, sl], preferred_element_type=F32)
        a = (g * jax.nn.sigmoid(g) * up).astype(BF16)
        d = jnp.dot(a, wd_ref[sl, :], preferred_element_type=F32)
        acc = d if acc is None else acc + d
    y = h + acc
    out_ref[...] = _rms(y, onw_ref[...]) if final_norm else y


def _ffn_call(x2, hm, ha, wo, fnw, wg, wu, wd, onw, final_norm):
    T = x2.shape[0]
    tm = TM_FFN
    const = lambda i: (0, 0)
    tok = lambda i: (i, 0)
    once = pl.Buffered(1)
    return pl.pallas_call(
        functools.partial(_ffn_kernel, final_norm),
        grid=(T // tm,),
        in_specs=[
            pl.BlockSpec((tm, D_MODEL), tok),
            pl.BlockSpec((tm, M_WIDTH), tok),
            pl.BlockSpec((tm, A_WIDTH), tok),
            pl.BlockSpec((M_WIDTH + A_WIDTH, D_MODEL), const, pipeline_mode=once),
            pl.BlockSpec((1, D_MODEL), const),
            pl.BlockSpec((D_MODEL, D_FF), const, pipeline_mode=once),
            pl.BlockSpec((D_MODEL, D_FF), const, pipeline_mode=once),
            pl.BlockSpec((D_FF, D_MODEL), const, pipeline_mode=once),
            pl.BlockSpec((1, D_MODEL), const),
        ],
        out_specs=pl.BlockSpec((tm, D_MODEL), tok),
        out_shape=jax.ShapeDtypeStruct((T, D_MODEL), F32),
        compiler_params=pltpu.CompilerParams(
            dimension_semantics=("arbitrary",), vmem_limit_bytes=VMEM_LIMIT),
        name="ffn",
    )(x2, hm, ha, wo, fnw, wg, wu, wd, onw)


def _prep_w_in(w_in):
    c = 4 * M_WIDTH
    mi = w_in[:, c:c + M_HEADS]
    mf = w_in[:, c + M_HEADS:c + 2 * M_HEADS]
    c += 2 * M_HEADS
    qa = w_in[:, c:c + Q_RANK]
    c += Q_RANK
    kva = w_in[:, c:c + KV_RANK]
    c += KV_RANK
    x1 = w_in[:, c:c + A_ROPE // 2]
    x2 = w_in[:, c + A_ROPE // 2:c + A_ROPE]
    zeros = jnp.zeros((D_MODEL, A_NOPE - 2 * M_HEADS), w_in.dtype)
    misc = jnp.concatenate([mi, mf, zeros, x1, x2, x2, x1], axis=1)
    mq, mk, mv, mo = (w_in[:, j * M_WIDTH:(j + 1) * M_WIDTH].astype(BF16) for j in range(4))
    return (mq, mv, mo, mk), jnp.concatenate([qa, kva, misc], axis=1).astype(BF16)


def _prep_w_q(w_q_b):
    w = w_q_b.reshape(Q_RANK, A_HEADS, A_NOPE + A_ROPE)
    half = A_ROPE // 2
    nope, x1, x2 = w[:, :, :A_NOPE], w[:, :, A_NOPE:A_NOPE + half], w[:, :, A_NOPE + half:]
    return jnp.concatenate([nope, x1, x2, x2, x1], axis=2).reshape(Q_RANK, A_PAD).astype(BF16)


def _prep_w_kv(w_kv_b):
    w = w_kv_b.reshape(KV_RANK, A_HEADS, A_NOPE + A_V)
    z = jnp.zeros((KV_RANK, A_HEADS, HEAD_PAD - A_NOPE), w.dtype)
    kpart = jnp.concatenate([w[:, :, :A_NOPE], z], axis=2).reshape(KV_RANK, A_PAD)
    vpart = jnp.concatenate([w[:, :, A_NOPE:], z], axis=2).reshape(KV_RANK, A_PAD)
    return kpart.astype(BF16), vpart.T.astype(BF16)


def _rope_consts():
    half = A_ROPE // 2
    inv_freq = (ROPE_THETA ** (-jnp.arange(half, dtype=F32) / half)).reshape(half, 1)
    lane = jnp.arange(HEAD_PAD)
    kmask = (lane >= A_NOPE).astype(F32).reshape(1, HEAD_PAD)
    v_one = jnp.tile((lane >= A_V).astype(F32), A_HEADS).reshape(A_PAD, 1)
    return inv_freq, kmask, v_one


def kernel(x, positions, attn_norm_w, w_in, b_gates, mlstm_norm_w, q_a_norm_w, w_q_b, kv_a_norm_w, w_kv_b,
           w_out, ffn_norm_w, w_gate, w_up, w_down, final_norm_w):
    B, S, _ = x.shape
    T = B * S
    depth = w_in.shape[0]
    inv_freq, kmask, v_one = _rope_consts()
    pos3 = positions.reshape(T // TM_PROJ, 1, TM_PROJ)
    h = x.reshape(T, D_MODEL)
    for l in range(depth):
        wk, wvt = _prep_w_kv(w_kv_b[l])
        wm, wlat = _prep_w_in(w_in[l])
        mqvo, kt, gt, q, k, vt = _proj_call(
            h, pos3, attn_norm_w[l].reshape(1, D_MODEL), wm, wlat,
            q_a_norm_w[l].reshape(1, Q_RANK), _prep_w_q(w_q_b[l]),
            kv_a_norm_w[l].reshape(1, KV_RANK), wk, wvt, v_one, inv_freq, kmask)
        hm = _mlstm_call(mqvo, kt, gt, b_gates[l].reshape(2 * M_HEADS, 1),
                         mlstm_norm_w[l].reshape(1, M_WIDTH), B, S)
        ha = _attn_call(q, k, vt, B, S)
        h = _ffn_call(h, hm, ha, w_out[l].astype(BF16), ffn_norm_w[l].reshape(1, D_MODEL),
                      w_gate[l].astype(BF16), w_up[l].astype(BF16), w_down[l].astype(BF16),
                      final_norm_w.reshape(1, D_MODEL), final_norm=(l == depth - 1))
    return h.reshape(B, S, D_MODEL)
```

```python
import functools
import math

import jax
import jax.numpy as jnp
from jax import lax
from jax.experimental import pallas as pl
from jax.experimental.pallas import tpu as pltpu

F32 = jnp.float32
BF16 = jnp.bfloat16

D_MODEL = 1024
M_HEADS = 4
M_HEAD_DIM = 128
M_WIDTH = M_HEADS * M_HEAD_DIM
A_HEADS = 8
A_NOPE = 64
A_ROPE = 32
A_V = 64
A_WIDTH = A_HEADS * A_V
Q_RANK = 384
KV_RANK = 256
ROPE_THETA = 10000.0
D_FF = 2816
EPS = 1e-6

LANE = 128
HEAD_PAD = LANE
A_PAD = A_HEADS * HEAD_PAD
MISC = LANE
D_LAT = Q_RANK + KV_RANK + MISC
VMEM_LIMIT = 56 * 1024 * 1024

TM_PROJ = 512
TM_FFN = 1024
M_CHUNK = 256
TQ = 512
A_GROUP = 8
FF_CHUNK = 256

M_SCALE = M_HEAD_DIM ** -0.5
A_SCALE2 = (A_NOPE + A_ROPE) ** -0.5 * math.log2(math.e)
NEG = -1e30


def _rms(x, w):
    return x * lax.rsqrt(jnp.mean(x * x, axis=-1, keepdims=True) + EPS) * w


def _proj_kernel(x_ref, pos_ref, nw_ref, wmq_ref, wmv_ref, wmo_ref, wmk_ref, wlat_ref, qnw_ref, wq_ref, kvnw_ref,
                 wk_ref, wvt_ref, vone_ref, invf_ref, kmask_ref, mqvo_ref, kt_ref, g_ref, q_ref, k_ref, vt_ref):
    tm = x_ref.shape[0]
    x = x_ref[...]
    u = _rms(x, nw_ref[...]).astype(BF16)

    def proj(w_ref):
        return jnp.dot(u, w_ref[...], preferred_element_type=F32)

    pb = proj(wlat_ref)
    qa = pb[:, 0:Q_RANK]
    kva = pb[:, Q_RANK:Q_RANK + KV_RANK]
    misc = pb[:, Q_RANK + KV_RANK:Q_RANK + KV_RANK + MISC]

    ang = invf_ref[...] * pos_ref[0].astype(F32)
    cos = jnp.cos(ang)
    sin = jnp.sin(ang)
    half = A_ROPE // 2
    one = jnp.ones((A_NOPE, tm), F32)
    z_lo = jnp.zeros((A_NOPE, tm), F32)
    z_hi = jnp.zeros((HEAD_PAD - A_NOPE - 2 * half, tm), F32)
    a_tab = jnp.concatenate([one, cos, cos, z_hi], axis=0).T
    b_tab = jnp.concatenate([z_lo, -sin, sin, z_hi], axis=0).T

    pq = proj(wmq_ref)

    qn = _rms(qa, qnw_ref[...] * A_SCALE2).astype(BF16)
    kvn = _rms(kva, kvnw_ref[...])
    cq = jnp.dot(qn, wq_ref[...], preferred_element_type=F32)
    ck = jnp.dot(kvn.astype(BF16), wk_ref[...], preferred_element_type=F32)
    vt = jnp.dot(wvt_ref[...], kvn.T.astype(BF16), preferred_element_type=F32)
    mqvo_ref[:, 0:M_WIDTH] = (pq * M_SCALE).astype(BF16)

    pv = proj(wmv_ref)
    for h in range(A_HEADS):
        sl = slice(h * HEAD_PAD, (h + 1) * HEAD_PAD)
        qh = cq[:, sl]
        q_ref[:, sl] = (qh * a_tab + pltpu.roll(qh, 96, 1) * b_tab).astype(BF16)

    po = proj(wmo_ref)
    g_ref[...] = misc.T[0:2 * M_HEADS, :]
    krin = misc * kmask_ref[...]
    kr = krin * a_tab + pltpu.roll(krin, 96, 1) * b_tab
    for h in range(A_HEADS):
        sl = slice(h * HEAD_PAD, (h + 1) * HEAD_PAD)
        k_ref[:, sl] = (ck[:, sl] + kr).astype(BF16)
    vt_ref[0] = (vt + vone_ref[...]).astype(BF16)

    pk = proj(wmk_ref)
    mqvo_ref[:, M_WIDTH:2 * M_WIDTH] = pv.astype(BF16)
    mqvo_ref[:, 2 * M_WIDTH:3 * M_WIDTH] = po.astype(BF16)
    kt_ref[...] = pk.T.astype(BF16)


def _proj_call(x2, pos3, nw, wm, wlat, qnw, wq, kvnw, wk, wvt, vone, invf, kmask):
    T = x2.shape[0]
    tm = TM_PROJ
    assert tm == TQ, "the transposed value blocks are written one attention key block per projection tile"
    const = lambda i: (0, 0)
    tok = lambda i: (i, 0)
    tokt = lambda i: (0, i)
    return pl.pallas_call(
        _proj_kernel,
        grid=(T // tm,),
        in_specs=[
            pl.BlockSpec((tm, D_MODEL), tok),
            pl.BlockSpec((1, 1, tm), lambda i: (i, 0, 0)),
            pl.BlockSpec((1, D_MODEL), const),
            pl.BlockSpec((D_MODEL, M_WIDTH), const),
            pl.BlockSpec((D_MODEL, M_WIDTH), const),
            pl.BlockSpec((D_MODEL, M_WIDTH), const),
            pl.BlockSpec((D_MODEL, M_WIDTH), const),
            pl.BlockSpec((D_MODEL, D_LAT), const),
            pl.BlockSpec((1, Q_RANK), const),
            pl.BlockSpec((Q_RANK, A_PAD), const),
            pl.BlockSpec((1, KV_RANK), const),
            pl.BlockSpec((KV_RANK, A_PAD), const),
            pl.BlockSpec((A_PAD, KV_RANK), const),
            pl.BlockSpec((A_PAD, 1), const),
            pl.BlockSpec((A_ROPE // 2, 1), const),
            pl.BlockSpec((1, HEAD_PAD), const),
        ],
        out_specs=[
            pl.BlockSpec((tm, 3 * M_WIDTH), tok),
            pl.BlockSpec((M_WIDTH, tm), tokt),
            pl.BlockSpec((2 * M_HEADS, tm), tokt),
            pl.BlockSpec((tm, A_PAD), tok),
            pl.BlockSpec((tm, A_PAD), tok),
            pl.BlockSpec((1, A_PAD, tm), lambda i: (i, 0, 0)),
        ],
        out_shape=[
            jax.ShapeDtypeStruct((T, 3 * M_WIDTH), BF16),
            jax.ShapeDtypeStruct((M_WIDTH, T), BF16),
            jax.ShapeDtypeStruct((2 * M_HEADS, T), F32),
            jax.ShapeDtypeStruct((T, A_PAD), BF16),
            jax.ShapeDtypeStruct((T, A_PAD), BF16),
            jax.ShapeDtypeStruct((T // tm, A_PAD, tm), BF16),
        ],
        compiler_params=pltpu.CompilerParams(
            dimension_semantics=("arbitrary",), vmem_limit_bytes=VMEM_LIMIT),
        name="proj",
    )(x2, pos3, nw, *wm, wlat, qnw, wq, kvnw, wk, wvt, vone, invf, kmask)


def _mlstm_kernel(q_ref, v_ref, o_ref, kt_ref, g_ref, bias_ref, nw_ref, out_ref, c_ref, m_ref):
    L = M_CHUNK

    @pl.when(pl.program_id(1) == 0)
    def _():
        c_ref[...] = jnp.zeros_like(c_ref)
        m_ref[...] = jnp.zeros_like(m_ref)

    x = g_ref[...] + bias_ref[...]
    lf = jax.nn.log_sigmoid(x)
    row = lax.broadcasted_iota(jnp.int32, (L, L), 0)
    col = lax.broadcasted_iota(jnp.int32, (L, L), 1)
    causal = col <= row
    p1 = lf.astype(BF16)
    r1 = lf - p1.astype(F32)
    p2 = r1.astype(BF16)
    p3 = (r1 - p2.astype(F32)).astype(BF16)
    triu = (row <= col).astype(BF16)
    cum = jnp.dot(jnp.concatenate([p1, p2, p3], axis=0), triu, preferred_element_type=F32)
    b_rows = cum[0:8] + cum[8:16] + cum[16:24]
    ones = jnp.ones((L, M_HEAD_DIM), BF16)

    for h in range(M_HEADS):
        sl = slice(h * M_HEAD_DIM, (h + 1) * M_HEAD_DIM)
        b_row = b_rows[M_HEADS + h:M_HEADS + h + 1, :]
        lf_row = lf[M_HEADS + h:M_HEADS + h + 1, :]
        c_row = x[h:h + 1, :] - b_row
        g_tot = b_row[:, L - 1:L]
        m_prev = m_ref[h][0:1, 0:1]

        qh = q_ref[:, sl]
        kt = kt_ref[sl, :]
        vx = jnp.concatenate([v_ref[:, sl], ones], axis=1)

        a_row = g_tot + c_row
        m_loc = jnp.max(a_row, axis=1, keepdims=True)
        w_row = jnp.exp(a_row - m_loc)
        c_loc = jnp.dot((kt.astype(F32) * w_row).astype(BF16), vx, preferred_element_type=F32)

        cmat = jnp.where(causal, c_row, NEG)
        mm_t = jnp.maximum(jnp.max(cmat, axis=1, keepdims=True), m_prev)
        b_col = jnp.sum(jnp.where(causal, lf_row, 0.0), axis=1, keepdims=True)
        s = jnp.dot(qh, kt, preferred_element_type=F32)
        p = jnp.exp(cmat - mm_t) * s
        s_inter = jnp.exp(m_prev - mm_t)
        c_prev = c_ref[h]
        pv = jnp.dot(p.astype(BF16), vx, preferred_element_type=F32)
        qc = jnp.dot(qh, c_prev.astype(BF16), preferred_element_type=F32)
        num = pv[:, 0:M_HEAD_DIM] + s_inter * qc[:, 0:M_HEAD_DIM]
        den = pv[:, M_HEAD_DIM:M_HEAD_DIM + 1] + s_inter * qc[:, M_HEAD_DIM:M_HEAD_DIM + 1]
        den = jnp.maximum(jnp.abs(den), jnp.exp(-(b_col + mm_t)))
        hh = num * (1.0 / den)
        hn = _rms(hh, nw_ref[:, sl])
        out_ref[:, sl] = (hn * jax.nn.sigmoid(o_ref[:, sl].astype(F32))).astype(BF16)

        m_new = jnp.maximum(g_tot + m_prev, m_loc)
        s_old = jnp.exp(g_tot + m_prev - m_new)
        s_loc = jnp.exp(m_loc - m_new)
        c_ref[h] = s_old * c_prev + s_loc * c_loc
        m_ref[h] = jnp.broadcast_to(m_new, (8, LANE))


def _mlstm_call(mqvo, kt, gt, bias, nw, B, S):
    T = B * S
    L = M_CHUNK
    nc = S // L
    blk = lambda j: (lambda b, c: (b * nc + c, j))
    blkt = lambda b, c: (0, b * nc + c)
    const = lambda b, c: (0, 0)
    return pl.pallas_call(
        _mlstm_kernel,
        grid=(B, nc),
        in_specs=[
            pl.BlockSpec((L, M_WIDTH), blk(0)),
            pl.BlockSpec((L, M_WIDTH), blk(1)),
            pl.BlockSpec((L, M_WIDTH), blk(2)),
            pl.BlockSpec((M_WIDTH, L), blkt),
            pl.BlockSpec((2 * M_HEADS, L), blkt),
            pl.BlockSpec((2 * M_HEADS, 1), const),
            pl.BlockSpec((1, M_WIDTH), const),
        ],
        out_specs=pl.BlockSpec((L, M_WIDTH), blk(0)),
        out_shape=jax.ShapeDtypeStruct((T, M_WIDTH), BF16),
        scratch_shapes=[
            pltpu.VMEM((M_HEADS, M_HEAD_DIM, 2 * M_HEAD_DIM), F32),
            pltpu.VMEM((M_HEADS, 8, LANE), F32),
        ],
        compiler_params=pltpu.CompilerParams(
            dimension_semantics=("arbitrary", "arbitrary"), vmem_limit_bytes=VMEM_LIMIT),
        name="mlstm",
    )(mqvo, mqvo, mqvo, kt, gt, bias, nw)


def _attn_kernel(q_ref, k_ref, vt_ref, o_ref):
    i = pl.program_id(2)
    tq = TQ
    hq = tq // 2
    nt = (((1,), (1,)), ((), ()))
    chains = [(h, c) for h in range(A_GROUP) for c in range(2)]
    qs = [q_ref[c * hq:(c + 1) * hq, h * HEAD_PAD:(h + 1) * HEAD_PAD] for h, c in chains]

    def step(j, carry, diag):
        start = pl.multiple_of(j * tq, tq)
        sts = []
        for n, (h, c) in enumerate(chains):
            nk = hq if (diag and c == 0) else tq
            kj = k_ref[pl.ds(start, nk), h * HEAD_PAD:(h + 1) * HEAD_PAD]
            st = lax.dot_general(kj, qs[n], nt, preferred_element_type=F32)
            if diag:
                key = lax.broadcasted_iota(jnp.int32, (nk, hq), 0)
                qry = lax.broadcasted_iota(jnp.int32, (nk, hq), 1) + c * hq
                st = jnp.where(key <= qry, st, NEG)
            sts.append(st)
        soft = []
        for n in range(len(chains)):
            m = carry[n][0]
            m_new = jnp.maximum(m, jnp.max(sts[n], axis=0, keepdims=True))
            soft.append((m_new, jnp.exp2(m - m_new), jnp.exp2(sts[n] - m_new).astype(BF16)))
        new = []
        for n, (h, c) in enumerate(chains):
            m_new, alpha, pt = soft[n]
            vtj = vt_ref[j, h * HEAD_PAD:(h + 1) * HEAD_PAD, 0:pt.shape[0]]
            new.append((m_new, alpha * carry[n][1] + jnp.dot(vtj, pt, preferred_element_type=F32)))
        return tuple(new)

    init = tuple((jnp.full((1, hq), NEG, F32), jnp.zeros((HEAD_PAD, hq), F32)) for _ in chains)
    carry = lax.fori_loop(0, i, lambda j, c: step(j, c, False), init)
    carry = step(i, carry, True)

    for hp in range(A_GROUP // 2):
        ot = [jnp.concatenate([carry[2 * (2 * hp + e)][1], carry[2 * (2 * hp + e) + 1][1]], axis=1) for e in range(2)]
        ot = [a[0:A_V, :] * (1.0 / a[A_V:A_V + 1, :]) for a in ot]
        o_ref[:, hp * HEAD_PAD:(hp + 1) * HEAD_PAD] = jnp.concatenate(ot, axis=0).T.astype(BF16)


def _attn_call(q, k, vt, B, S):
    T = B * S
    tq = TQ
    nq = S // tq
    gw = A_GROUP * HEAD_PAD
    return pl.pallas_call(
        _attn_kernel,
        grid=(B, A_HEADS // A_GROUP, nq),
        in_specs=[
            pl.BlockSpec((tq, gw), lambda b, g, i: (b * nq + i, g)),
            pl.BlockSpec((S, gw), lambda b, g, i: (b, g)),
            pl.BlockSpec((nq, gw, tq), lambda b, g, i: (b, g, 0)),
        ],
        out_specs=pl.BlockSpec((tq, A_GROUP * A_V), lambda b, g, i: (b * nq + i, g)),
        out_shape=jax.ShapeDtypeStruct((T, A_WIDTH), BF16),
        compiler_params=pltpu.CompilerParams(
            dimension_semantics=("arbitrary", "arbitrary", "arbitrary"), vmem_limit_bytes=VMEM_LIMIT),
        name="mla_attn",
    )(q, k, vt)


def _ffn_kernel(final_norm, x_ref, hm_ref, ha_ref, wo_ref, fnw_ref, wg_ref, wu_ref, wd_ref, onw_ref, out_ref):
    h = (x_ref[...]
         + jnp.dot(hm_ref[...], wo_ref[0:M_WIDTH, :], preferred_element_type=F32)
         + jnp.dot(ha_ref[...], wo_ref[M_WIDTH:M_WIDTH + A_WIDTH, :], preferred_element_type=F32))
    u = _rms(h, fnw_ref[...]).astype(BF16)
    acc = None
    for c in range(D_FF // FF_CHUNK):
        sl = slice(c * FF_CHUNK, (c + 1) * FF_CHUNK)
        g = jnp.dot(u, wg_ref[:, sl], preferred_element_type=F32)
        up = jnp.dot(u, wu_ref[:, sl], preferred_element_type=F32)
        a = (g * jax.nn.sigmoid(g) * up).astype(BF16)
        d = jnp.dot(a, wd_ref[sl, :], preferred_element_type=F32)
        acc = d if acc is None else acc + d
    y = h + acc
    out_ref[...] = _rms(y, onw_ref[...]) if final_norm else y


def _ffn_call(x2, hm, ha, wo, fnw, wg, wu, wd, onw, final_norm):
    T = x2.shape[0]
    tm = TM_FFN
    const = lambda i: (0, 0)
    tok = lambda i: (i, 0)
    once = pl.Buffered(1)
    return pl.pallas_call(
        functools.partial(_ffn_kernel, final_norm),
        grid=(T // tm,),
        in_specs=[
            pl.BlockSpec((tm, D_MODEL), tok),
            pl.BlockSpec((tm, M_WIDTH), tok),
            pl.BlockSpec((tm, A_WIDTH), tok),
            pl.BlockSpec((M_WIDTH + A_WIDTH, D_MODEL), const, pipeline_mode=once),
            pl.BlockSpec((1, D_MODEL), const),
            pl.BlockSpec((D_MODEL, D_FF), const, pipeline_mode=once),
            pl.BlockSpec((D_MODEL, D_FF), const, pipeline_mode=once),
            pl.BlockSpec((D_FF, D_MODEL), const, pipeline_mode=once),
            pl.BlockSpec((1, D_MODEL), const),
        ],
        out_specs=pl.BlockSpec((tm, D_MODEL), tok),
        out_shape=jax.ShapeDtypeStruct((T, D_MODEL), F32),
        compiler_params=pltpu.CompilerParams(
            dimension_semantics=("arbitrary",), vmem_limit_bytes=VMEM_LIMIT),
        name="ffn",
    )(x2, hm, ha, wo, fnw, wg, wu, wd, onw)


def _prep_w_in(w_in):
    c = 4 * M_WIDTH
    mi = w_in[:, c:c + M_HEADS]
    mf = w_in[:, c + M_HEADS:c + 2 * M_HEADS]
    c += 2 * M_HEADS
    qa = w_in[:, c:c + Q_RANK]
    c += Q_RANK
    kva = w_in[:, c:c + KV_RANK]
    c += KV_RANK
    x1 = w_in[:, c:c + A_ROPE // 2]
    x2 = w_in[:, c + A_ROPE // 2:c + A_ROPE]
    zeros = jnp.zeros((D_MODEL, A_NOPE - 2 * M_HEADS), w_in.dtype)
    misc = jnp.concatenate([mi, mf, zeros, x1, x2, x2, x1], axis=1)
    mq, mk, mv, mo = (w_in[:, j * M_WIDTH:(j + 1) * M_WIDTH].astype(BF16) for j in range(4))
    return (mq, mv, mo, mk), jnp.concatenate([qa, kva, misc], axis=1).astype(BF16)


def _prep_w_q(w_q_b):
    w = w_q_b.reshape(Q_RANK, A_HEADS, A_NOPE + A_ROPE)
    half = A_ROPE // 2
    nope, x1, x2 = w[:, :, :A_NOPE], w[:, :, A_NOPE:A_NOPE + half], w[:, :, A_NOPE + half:]
    return jnp.concatenate([nope, x1, x2, x2, x1], axis=2).reshape(Q_RANK, A_PAD).astype(BF16)


def _prep_w_kv(w_kv_b):
    w = w_kv_b.reshape(KV_RANK, A_HEADS, A_NOPE + A_V)
    z = jnp.zeros((KV_RANK, A_HEADS, HEAD_PAD - A_NOPE), w.dtype)
    kpart = jnp.concatenate([w[:, :, :A_NOPE], z], axis=2).reshape(KV_RANK, A_PAD)
    vpart = jnp.concatenate([w[:, :, A_NOPE:], z], axis=2).reshape(KV_RANK, A_PAD)
    return kpart.astype(BF16), vpart.T.astype(BF16)


def _rope_consts():
    half = A_ROPE // 2
    inv_freq = (ROPE_THETA ** (-jnp.arange(half, dtype=F32) / half)).reshape(half, 1)
    lane = jnp.arange(HEAD_PAD)
    kmask = (lane >= A_NOPE).astype(F32).reshape(1, HEAD_PAD)
    v_one = jnp.tile((lane >= A_V).astype(F32), A_HEADS).reshape(A_PAD, 1)
    return inv_freq, kmask, v_one


def kernel(x, positions, attn_norm_w, w_in, b_gates, mlstm_norm_w, q_a_norm_w, w_q_b, kv_a_norm_w, w_kv_b,
           w_out, ffn_norm_w, w_gate, w_up, w_down, final_norm_w):
    B, S, _ = x.shape
    T = B * S
    depth = w_in.shape[0]
    inv_freq, kmask, v_one = _rope_consts()
    pos3 = positions.reshape(T // TM_PROJ, 1, TM_PROJ)
    h = x.reshape(T, D_MODEL)
    for l in range(depth):
        wk, wvt = _prep_w_kv(w_kv_b[l])
        wm, wlat = _prep_w_in(w_in[l])
        mqvo, kt, gt, q, k, vt = _proj_call(
            h, pos3, attn_norm_w[l].reshape(1, D_MODEL), wm, wlat,
            q_a_norm_w[l].reshape(1, Q_RANK), _prep_w_q(w_q_b[l]),
            kv_a_norm_w[l].reshape(1, KV_RANK), wk, wvt, v_one, inv_freq, kmask)
        hm = _mlstm_call(mqvo, kt, gt, b_gates[l].reshape(2 * M_HEADS, 1),
                         mlstm_norm_w[l].reshape(1, M_WIDTH), B, S)
        ha = _attn_call(q, k, vt, B, S)
        h = _ffn_call(h, hm, ha, w_out[l].astype(BF16), ffn_norm_w[l].reshape(1, D_MODEL),
                      w_gate[l].astype(BF16), w_up[l].astype(BF16), w_down[l].astype(BF16),
                      final_norm_w.reshape(1, D_MODEL), final_norm=(l == depth - 1))
    return h.reshape(B, S, D_MODEL)
```

```python
import functools
import math

import jax
import jax.numpy as jnp
from jax import lax
from jax.experimental import pallas as pl
from jax.experimental.pallas import tpu as pltpu

F32 = jnp.float32
BF16 = jnp.bfloat16

D_MODEL = 1024
M_HEADS = 4
M_HEAD_DIM = 128
M_WIDTH = M_HEADS * M_HEAD_DIM
A_HEADS = 8
A_NOPE = 64
A_ROPE = 32
A_V = 64
A_WIDTH = A_HEADS * A_V
Q_RANK = 384
KV_RANK = 256
ROPE_THETA = 10000.0
D_FF = 2816
EPS = 1e-6

LANE = 128
HEAD_PAD = LANE
A_PAD = A_HEADS * HEAD_PAD
MISC = LANE
D_LAT = Q_RANK + KV_RANK + MISC
VMEM_LIMIT = 56 * 1024 * 1024

TM_PROJ = 512
TM_FFN = 1024
M_CHUNK = 256
TQ = 512
A_GROUP = 8
A_LAG = 3
FF_CHUNK = 256

M_SCALE = M_HEAD_DIM ** -0.5
A_SCALE2 = (A_NOPE + A_ROPE) ** -0.5 * math.log2(math.e)
NEG = -1e30


def _rms(x, w):
    return x * lax.rsqrt(jnp.mean(x * x, axis=-1, keepdims=True) + EPS) * w


def _proj_kernel(x_ref, pos_ref, nw_ref, wmq_ref, wmv_ref, wmo_ref, wmk_ref, wlat_ref, qnw_ref, wq_ref, kvnw_ref,
                 wk_ref, wvt_ref, vone_ref, invf_ref, kmask_ref, mqvo_ref, kt_ref, g_ref, q_ref, k_ref, vt_ref):
    tm = x_ref.shape[0]
    x = x_ref[...]
    u = _rms(x, nw_ref[...]).astype(BF16)

    def proj(w_ref):
        return jnp.dot(u, w_ref[...], preferred_element_type=F32)

    pb = proj(wlat_ref)
    qa = pb[:, 0:Q_RANK]
    kva = pb[:, Q_RANK:Q_RANK + KV_RANK]
    misc = pb[:, Q_RANK + KV_RANK:Q_RANK + KV_RANK + MISC]

    ang = invf_ref[...] * pos_ref[0].astype(F32)
    cos = jnp.cos(ang)
    sin = jnp.sin(ang)
    half = A_ROPE // 2
    one = jnp.ones((A_NOPE, tm), F32)
    z_lo = jnp.zeros((A_NOPE, tm), F32)
    z_hi = jnp.zeros((HEAD_PAD - A_NOPE - 2 * half, tm), F32)
    a_tab = jnp.concatenate([one, cos, cos, z_hi], axis=0).T
    b_tab = jnp.concatenate([z_lo, -sin, sin, z_hi], axis=0).T

    pq = proj(wmq_ref)

    qn = _rms(qa, qnw_ref[...] * A_SCALE2).astype(BF16)
    kvn = _rms(kva, kvnw_ref[...])
    cq = jnp.dot(qn, wq_ref[...], preferred_element_type=F32)
    ck = jnp.dot(kvn.astype(BF16), wk_ref[...], preferred_element_type=F32)
    vt = jnp.dot(wvt_ref[...], kvn.T.astype(BF16), preferred_element_type=F32)
    mqvo_ref[:, 0:M_WIDTH] = (pq * M_SCALE).astype(BF16)

    pk = proj(wmk_ref)
    for h in range(A_HEADS):
        sl = slice(h * HEAD_PAD, (h + 1) * HEAD_PAD)
        qh = cq[:, sl]
        q_ref[:, sl] = (qh * a_tab + pltpu.roll(qh, 96, 1) * b_tab).astype(BF16)

    pv = proj(wmv_ref)
    kt_ref[...] = pk.T.astype(BF16)
    g_ref[...] = misc.T[0:2 * M_HEADS, :]
    krin = misc * kmask_ref[...]
    kr = krin * a_tab + pltpu.roll(krin, 96, 1) * b_tab
    for h in range(A_HEADS):
        sl = slice(h * HEAD_PAD, (h + 1) * HEAD_PAD)
        k_ref[:, sl] = (ck[:, sl] + kr).astype(BF16)
    vt_ref[0] = (vt + vone_ref[...]).astype(BF16)

    po = proj(wmo_ref)
    mqvo_ref[:, M_WIDTH:2 * M_WIDTH] = pv.astype(BF16)
    mqvo_ref[:, 2 * M_WIDTH:3 * M_WIDTH] = po.astype(BF16)


def _proj_call(x2, pos3, nw, wm, wlat, qnw, wq, kvnw, wk, wvt, vone, invf, kmask):
    T = x2.shape[0]
    tm = TM_PROJ
    assert tm == TQ, "the transposed value blocks are written one attention key block per projection tile"
    const = lambda i: (0, 0)
    tok = lambda i: (i, 0)
    tokt = lambda i: (0, i)
    return pl.pallas_call(
        _proj_kernel,
        grid=(T // tm,),
        in_specs=[
            pl.BlockSpec((tm, D_MODEL), tok),
            pl.BlockSpec((1, 1, tm), lambda i: (i, 0, 0)),
            pl.BlockSpec((1, D_MODEL), const),
            pl.BlockSpec((D_MODEL, M_WIDTH), const),
            pl.BlockSpec((D_MODEL, M_WIDTH), const),
            pl.BlockSpec((D_MODEL, M_WIDTH), const),
            pl.BlockSpec((D_MODEL, M_WIDTH), const),
            pl.BlockSpec((D_MODEL, D_LAT), const),
            pl.BlockSpec((1, Q_RANK), const),
            pl.BlockSpec((Q_RANK, A_PAD), const),
            pl.BlockSpec((1, KV_RANK), const),
            pl.BlockSpec((KV_RANK, A_PAD), const),
            pl.BlockSpec((A_PAD, KV_RANK), const),
            pl.BlockSpec((A_PAD, 1), const),
            pl.BlockSpec((A_ROPE // 2, 1), const),
            pl.BlockSpec((1, HEAD_PAD), const),
        ],
        out_specs=[
            pl.BlockSpec((tm, 3 * M_WIDTH), tok),
            pl.BlockSpec((M_WIDTH, tm), tokt),
            pl.BlockSpec((2 * M_HEADS, tm), tokt),
            pl.BlockSpec((tm, A_PAD), tok),
            pl.BlockSpec((tm, A_PAD), tok),
            pl.BlockSpec((1, A_PAD, tm), lambda i: (i, 0, 0)),
        ],
        out_shape=[
            jax.ShapeDtypeStruct((T, 3 * M_WIDTH), BF16),
            jax.ShapeDtypeStruct((M_WIDTH, T), BF16),
            jax.ShapeDtypeStruct((2 * M_HEADS, T), F32),
            jax.ShapeDtypeStruct((T, A_PAD), BF16),
            jax.ShapeDtypeStruct((T, A_PAD), BF16),
            jax.ShapeDtypeStruct((T // tm, A_PAD, tm), BF16),
        ],
        compiler_params=pltpu.CompilerParams(
            dimension_semantics=("arbitrary",), vmem_limit_bytes=VMEM_LIMIT),
        name="proj",
    )(x2, pos3, nw, *wm, wlat, qnw, wq, kvnw, wk, wvt, vone, invf, kmask)


def _mlstm_kernel(q_ref, v_ref, o_ref, kt_ref, g_ref, bias_ref, nw_ref, out_ref, c_ref, m_ref):
    L = M_CHUNK

    @pl.when(pl.program_id(1) == 0)
    def _():
        c_ref[...] = jnp.zeros_like(c_ref)
        m_ref[...] = jnp.zeros_like(m_ref)

    x = g_ref[...] + bias_ref[...]
    lf = jax.nn.log_sigmoid(x)
    row = lax.broadcasted_iota(jnp.int32, (L, L), 0)
    col = lax.broadcasted_iota(jnp.int32, (L, L), 1)
    causal = col <= row
    p1 = lf.astype(BF16)
    r1 = lf - p1.astype(F32)
    p2 = r1.astype(BF16)
    p3 = (r1 - p2.astype(F32)).astype(BF16)
    triu = (row <= col).astype(BF16)
    cum = jnp.dot(jnp.concatenate([p1, p2, p3], axis=0), triu, preferred_element_type=F32)
    b_rows = cum[0:8] + cum[8:16] + cum[16:24]
    ones = jnp.ones((L, M_HEAD_DIM), BF16)

    for h in range(M_HEADS):
        sl = slice(h * M_HEAD_DIM, (h + 1) * M_HEAD_DIM)
        b_row = b_rows[M_HEADS + h:M_HEADS + h + 1, :]
        lf_row = lf[M_HEADS + h:M_HEADS + h + 1, :]
        c_row = x[h:h + 1, :] - b_row
        g_tot = b_row[:, L - 1:L]
        m_prev = m_ref[h][0:1, 0:1]

        qh = q_ref[:, sl]
        kt = kt_ref[sl, :]
        vx = jnp.concatenate([v_ref[:, sl], ones], axis=1)

        a_row = g_tot + c_row
        m_loc = jnp.max(a_row, axis=1, keepdims=True)
        w_row = jnp.exp(a_row - m_loc)
        c_loc = jnp.dot((kt.astype(F32) * w_row).astype(BF16), vx, preferred_element_type=F32)

        cmat = jnp.where(causal, c_row, NEG)
        mm_t = jnp.maximum(jnp.max(cmat, axis=1, keepdims=True), m_prev)
        b_col = jnp.sum(jnp.where(causal, lf_row, 0.0), axis=1, keepdims=True)
        s = jnp.dot(qh, kt, preferred_element_type=F32)
        p = jnp.exp(cmat - mm_t) * s
        s_inter = jnp.exp(m_prev - mm_t)
        c_prev = c_ref[h]
        pv = jnp.dot(p.astype(BF16), vx, preferred_element_type=F32)
        qc = jnp.dot(qh, c_prev.astype(BF16), preferred_element_type=F32)
        num = pv[:, 0:M_HEAD_DIM] + s_inter * qc[:, 0:M_HEAD_DIM]
        den = pv[:, M_HEAD_DIM:M_HEAD_DIM + 1] + s_inter * qc[:, M_HEAD_DIM:M_HEAD_DIM + 1]
        den = jnp.maximum(jnp.abs(den), jnp.exp(-(b_col + mm_t)))
        hh = num * (1.0 / den)
        hn = _rms(hh, nw_ref[:, sl])
        out_ref[:, sl] = (hn * jax.nn.sigmoid(o_ref[:, sl].astype(F32))).astype(BF16)

        m_new = jnp.maximum(g_tot + m_prev, m_loc)
        s_old = jnp.exp(g_tot + m_prev - m_new)
        s_loc = jnp.exp(m_loc - m_new)
        c_ref[h] = s_old * c_prev + s_loc * c_loc
        m_ref[h] = jnp.broadcast_to(m_new, (8, LANE))


def _mlstm_call(mqvo, kt, gt, bias, nw, B, S):
    T = B * S
    L = M_CHUNK
    nc = S // L
    blk = lambda j: (lambda b, c: (b * nc + c, j))
    blkt = lambda b, c: (0, b * nc + c)
    const = lambda b, c: (0, 0)
    return pl.pallas_call(
        _mlstm_kernel,
        grid=(B, nc),
        in_specs=[
            pl.BlockSpec((L, M_WIDTH), blk(0)),
            pl.BlockSpec((L, M_WIDTH), blk(1)),
            pl.BlockSpec((L, M_WIDTH), blk(2)),
            pl.BlockSpec((M_WIDTH, L), blkt),
            pl.BlockSpec((2 * M_HEADS, L), blkt),
            pl.BlockSpec((2 * M_HEADS, 1), const),
            pl.BlockSpec((1, M_WIDTH), const),
        ],
        out_specs=pl.BlockSpec((L, M_WIDTH), blk(0)),
        out_shape=jax.ShapeDtypeStruct((T, M_WIDTH), BF16),
        scratch_shapes=[
            pltpu.VMEM((M_HEADS, M_HEAD_DIM, 2 * M_HEAD_DIM), F32),
            pltpu.VMEM((M_HEADS, 8, LANE), F32),
        ],
        compiler_params=pltpu.CompilerParams(
            dimension_semantics=("arbitrary", "arbitrary"), vmem_limit_bytes=VMEM_LIMIT),
        name="mlstm",
    )(mqvo, mqvo, mqvo, kt, gt, bias, nw)


def _attn_kernel(q_ref, k_ref, vt_ref, o_ref):
    i = pl.program_id(2)
    tq = TQ
    hq = tq // 2
    nt = (((1,), (1,)), ((), ()))
    chains = [(h, c) for h in range(A_GROUP) for c in range(2)]
    qs = [q_ref[c * hq:(c + 1) * hq, h * HEAD_PAD:(h + 1) * HEAD_PAD] for h, c in chains]

    def step(j, carry, diag):
        start = pl.multiple_of(j * tq, tq)
        if diag:
            tri = (lax.broadcasted_iota(jnp.int32, (hq, hq), 0)
                   <= lax.broadcasted_iota(jnp.int32, (hq, hq), 1))

        def scores(n):
            h, c = chains[n]
            nk = hq if (diag and c == 0) else tq
            kj = k_ref[pl.ds(start, nk), h * HEAD_PAD:(h + 1) * HEAD_PAD]
            st = lax.dot_general(kj, qs[n], nt, preferred_element_type=F32)
            if diag:
                tail = jnp.where(tri, st[nk - hq:nk, :], NEG)
                st = tail if nk == hq else jnp.concatenate([st[0:nk - hq, :], tail], axis=0)
            return st

        def softmax(n, st):
            m = carry[n][0]
            m_new = jnp.maximum(m, jnp.max(st, axis=0, keepdims=True))
            return m_new, jnp.exp2(m - m_new), jnp.exp2(st - m_new).astype(BF16)

        def values(n, m_new, alpha, pt):
            h, _ = chains[n]
            vtj = vt_ref[j, h * HEAD_PAD:(h + 1) * HEAD_PAD, 0:pt.shape[0]]
            return m_new, alpha * carry[n][1] + jnp.dot(vtj, pt, preferred_element_type=F32)

        nch = len(chains)
        sts, sms, new = {}, {}, [None] * nch
        for t in range(nch + 2 * A_LAG):
            if t < nch:
                sts[t] = scores(t)
            if 0 <= t - A_LAG < nch:
                sms[t - A_LAG] = softmax(t - A_LAG, sts.pop(t - A_LAG))
            if 0 <= t - 2 * A_LAG < nch:
                new[t - 2 * A_LAG] = values(t - 2 * A_LAG, *sms.pop(t - 2 * A_LAG))
        return tuple(new)

    init = tuple((jnp.full((1, hq), NEG, F32), jnp.zeros((HEAD_PAD, hq), F32)) for _ in chains)
    carry = lax.fori_loop(0, i, lambda j, c: step(j, c, False), init)
    carry = step(i, carry, True)

    for hp in range(A_GROUP // 2):
        ot = [jnp.concatenate([carry[2 * (2 * hp + e)][1], carry[2 * (2 * hp + e) + 1][1]], axis=1) for e in range(2)]
        ot = [a[0:A_V, :] * (1.0 / a[A_V:A_V + 1, :]) for a in ot]
        o_ref[:, hp * HEAD_PAD:(hp + 1) * HEAD_PAD] = jnp.concatenate(ot, axis=0).T.astype(BF16)


def _attn_call(q, k, vt, B, S):
    T = B * S
    tq = TQ
    nq = S // tq
    gw = A_GROUP * HEAD_PAD
    return pl.pallas_call(
        _attn_kernel,
        grid=(B, A_HEADS // A_GROUP, nq),
        in_specs=[
            pl.BlockSpec((tq, gw), lambda b, g, i: (b * nq + i, g)),
            pl.BlockSpec((S, gw), lambda b, g, i: (b, g)),
            pl.BlockSpec((nq, gw, tq), lambda b, g, i: (b, g, 0)),
        ],
        out_specs=pl.BlockSpec((tq, A_GROUP * A_V), lambda b, g, i: (b * nq + i, g)),
        out_shape=jax.ShapeDtypeStruct((T, A_WIDTH), BF16),
        compiler_params=pltpu.CompilerParams(
            dimension_semantics=("arbitrary", "arbitrary", "arbitrary"), vmem_limit_bytes=VMEM_LIMIT),
        name="mla_attn",
    )(q, k, vt)


def _ffn_kernel(final_norm, x_ref, hm_ref, ha_ref, wo_ref, fnw_ref, wg_ref, wu_ref, wd_ref, onw_ref, out_ref):
    assert D_FF % FF_CHUNK == 0
    n_slabs = D_FF // FF_CHUNK
    half = x_ref.shape[0] // 2
    rows = (slice(0, half), slice(half, 2 * half))

    def attn_out(r):
        return (x_ref[r, :]
                + jnp.dot(hm_ref[r, :], wo_ref[0:M_WIDTH, :], preferred_element_type=F32)
                + jnp.dot(ha_ref[r, :], wo_ref[M_WIDTH:M_WIDTH + A_WIDTH, :], preferred_element_type=F32))

    def slabs(u, acc, lo, hi):
        for c in range(lo, hi):
            sl = slice(c * FF_CHUNK, (c + 1) * FF_CHUNK)
            g = jnp.dot(u, wg_ref[:, sl], preferred_element_type=F32)
            up = jnp.dot(u, wu_ref[:, sl], preferred_element_type=F32)
            a = (g * jax.nn.sigmoid(g) * up).astype(BF16)
            d = jnp.dot(a, wd_ref[sl, :], preferred_element_type=F32)
            acc = d if acc is None else acc + d
        return acc

    def finish(r, h, acc):
        y = h + acc
        out_ref[r, :] = _rms(y, onw_ref[...]) if final_norm else y

    split = 3
    h0 = attn_out(rows[0])
    u0 = _rms(h0, fnw_ref[...]).astype(BF16)
    h1 = attn_out(rows[1])
    acc0 = slabs(u0, None, 0, split)
    u1 = _rms(h1, fnw_ref[...]).astype(BF16)
    acc0 = slabs(u0, acc0, split, n_slabs)
    acc1 = slabs(u1, None, 0, split)
    finish(rows[0], h0, acc0)
    acc1 = slabs(u1, acc1, split, n_slabs)
    finish(rows[1], h1, acc1)


def _ffn_call(x2, hm, ha, wo, fnw, wg, wu, wd, onw, final_norm):
    T = x2.shape[0]
    tm = TM_FFN
    const = lambda i: (0, 0)
    tok = lambda i: (i, 0)
    once = pl.Buffered(1)
    return pl.pallas_call(
        functools.partial(_ffn_kernel, final_norm),
        grid=(T // tm,),
        in_specs=[
            pl.BlockSpec((tm, D_MODEL), tok),
            pl.BlockSpec((tm, M_WIDTH), tok),
            pl.BlockSpec((tm, A_WIDTH), tok),
            pl.BlockSpec((M_WIDTH + A_WIDTH, D_MODEL), const, pipeline_mode=once),
            pl.BlockSpec((1, D_MODEL), const),
            pl.BlockSpec((D_MODEL, D_FF), const, pipeline_mode=once),
            pl.BlockSpec((D_MODEL, D_FF), const, pipeline_mode=once),
            pl.BlockSpec((D_FF, D_MODEL), const, pipeline_mode=once),
            pl.BlockSpec((1, D_MODEL), const),
        ],
        out_specs=pl.BlockSpec((tm, D_MODEL), tok),
        out_shape=jax.ShapeDtypeStruct((T, D_MODEL), F32),
        compiler_params=pltpu.CompilerParams(
            dimension_semantics=("arbitrary",), vmem_limit_bytes=VMEM_LIMIT),
        name="ffn",
    )(x2, hm, ha, wo, fnw, wg, wu, wd, onw)


def _prep_w_in(w_in):
    c = 4 * M_WIDTH
    mi = w_in[:, c:c + M_HEADS]
    mf = w_in[:, c + M_HEADS:c + 2 * M_HEADS]
    c += 2 * M_HEADS
    qa = w_in[:, c:c + Q_RANK]
    c += Q_RANK
    kva = w_in[:, c:c + KV_RANK]
    c += KV_RANK
    x1 = w_in[:, c:c + A_ROPE // 2]
    x2 = w_in[:, c + A_ROPE // 2:c + A_ROPE]
    zeros = jnp.zeros((D_MODEL, A_NOPE - 2 * M_HEADS), w_in.dtype)
    misc = jnp.concatenate([mi, mf, zeros, x1, x2, x2, x1], axis=1)
    mq, mk, mv, mo = (w_in[:, j * M_WIDTH:(j + 1) * M_WIDTH].astype(BF16) for j in range(4))
    return (mq, mv, mo, mk), jnp.concatenate([qa, kva, misc], axis=1).astype(BF16)


def _prep_w_q(w_q_b):
    w = w_q_b.reshape(Q_RANK, A_HEADS, A_NOPE + A_ROPE)
    half = A_ROPE // 2
    nope, x1, x2 = w[:, :, :A_NOPE], w[:, :, A_NOPE:A_NOPE + half], w[:, :, A_NOPE + half:]
    return jnp.concatenate([nope, x1, x2, x2, x1], axis=2).reshape(Q_RANK, A_PAD).astype(BF16)


def _prep_w_kv(w_kv_b):
    w = w_kv_b.reshape(KV_RANK, A_HEADS, A_NOPE + A_V)
    z = jnp.zeros((KV_RANK, A_HEADS, HEAD_PAD - A_NOPE), w.dtype)
    kpart = jnp.concatenate([w[:, :, :A_NOPE], z], axis=2).reshape(KV_RANK, A_PAD)
    vpart = jnp.concatenate([w[:, :, A_NOPE:], z], axis=2).reshape(KV_RANK, A_PAD)
    return kpart.astype(BF16), vpart.T.astype(BF16)


def _rope_consts():
    half = A_ROPE // 2
    inv_freq = (ROPE_THETA ** (-jnp.arange(half, dtype=F32) / half)).reshape(half, 1)
    lane = jnp.arange(HEAD_PAD)
    kmask = (lane >= A_NOPE).astype(F32).reshape(1, HEAD_PAD)
    v_one = jnp.tile((lane >= A_V).astype(F32), A_HEADS).reshape(A_PAD, 1)
    return inv_freq, kmask, v_one


def kernel(x, positions, attn_norm_w, w_in, b_gates, mlstm_norm_w, q_a_norm_w, w_q_b, kv_a_norm_w, w_kv_b,
           w_out, ffn_norm_w, w_gate, w_up, w_down, final_norm_w):
    B, S, _ = x.shape
    T = B * S
    depth = w_in.shape[0]
    inv_freq, kmask, v_one = _rope_consts()
    pos3 = positions.reshape(T // TM_PROJ, 1, TM_PROJ)
    h = x.reshape(T, D_MODEL)
    for l in range(depth):
        wk, wvt = _prep_w_kv(w_kv_b[l])
        wm, wlat = _prep_w_in(w_in[l])
        mqvo, kt, gt, q, k, vt = _proj_call(
            h, pos3, attn_norm_w[l].reshape(1, D_MODEL), wm, wlat,
            q_a_norm_w[l].reshape(1, Q_RANK), _prep_w_q(w_q_b[l]),
            kv_a_norm_w[l].reshape(1, KV_RANK), wk, wvt, v_one, inv_freq, kmask)
        hm = _mlstm_call(mqvo, kt, gt, b_gates[l].reshape(2 * M_HEADS, 1),
                         mlstm_norm_w[l].reshape(1, M_WIDTH), B, S)
        ha = _attn_call(q, k, vt, B, S)
        h = _ffn_call(h, hm, ha, w_out[l].astype(BF16), ffn_norm_w[l].reshape(1, D_MODEL),
                      w_gate[l].astype(BF16), w_up[l].astype(BF16), w_down[l].astype(BF16),
                      final_norm_w.reshape(1, D_MODEL), final_norm=(l == depth - 1))
    return h.reshape(B, S, D_MODEL)
```
